```python
import math
import jax
import jax.numpy as jnp
from jax import lax
import numpy as np

D_MODEL = 2048
BATCH = 1
SEQ = 8192
DEPTH = 2
DEC_BATCH = 128
DEC_SEQ = 4
PAST_LEN = 2048
PAGE_SIZE = 128

N_MEM = 256
EPS = 1e-6
N_EVEN = (DEPTH + 1) // 2
N_ODD = DEPTH // 2
POOL_WIDTH = D_MODEL // 4
POOL_GROUPS = 4
POOL_GROUP_DIM = POOL_WIDTH // POOL_GROUPS
POOL_WINDOWS = (2, 4, 8, 16)
POOL_BUF = max(POOL_WINDOWS) - 1
SB_WIDTH = D_MODEL - POOL_WIDTH
SB_HEAD_DIM = 128
SB_HEADS = SB_WIDTH // SB_HEAD_DIM
SB_SCALE = SB_HEAD_DIM ** -0.5
SB_BIAS_INIT = -8.0
Q_BLOCK = 128
EVEN_IN = POOL_WIDTH + 3 * SB_WIDTH
GLA_VW = D_MODEL // 2
GLA_HEADS = 4
GLA_DV = GLA_VW // GLA_HEADS
GLA_KW = GLA_VW // 2
GLA_DK = GLA_KW // GLA_HEADS
GLA_RANK = 16
GLA_TAU = 16.0
GLA_CHUNK = 32
GMLP_WIDTH = D_MODEL - GLA_VW
GMLP_GROUPS = 4
GMLP_GROUP_DIM = GMLP_WIDTH // GMLP_GROUPS
GMLP_CHUNK = 128
ODD_IN = 2 * GLA_KW + 2 * GLA_VW + GLA_RANK + 2 * GMLP_WIDTH
MEM_HEADS = 4
MEM_HEAD_DIM = 128
MEM_WIDTH = MEM_HEADS * MEM_HEAD_DIM
MEM_SCALE = MEM_HEAD_DIM ** -0.5
D_FF = 4 * D_MODEL

kernel_name = 'hybrid_pool_stickbreak_gla_gmlp_step'

F32 = jnp.float32


def rmsnorm(x, g):
    xf = x.astype(F32)
    y = xf * lax.rsqrt(jnp.mean(xf * xf, axis=-1, keepdims=True) + EPS)
    return (y * g.astype(F32)).astype(x.dtype)


def pool_mix(a, buf, pos0, w, scale):
    B, L, _ = a.shape
    ext = jnp.concatenate([buf.astype(F32), a.astype(F32)], axis=1)
    csum = jnp.concatenate([jnp.zeros((B, 1, POOL_WIDTH), F32), jnp.cumsum(ext, axis=1)], axis=1)
    pos = pos0 + jnp.arange(L)
    outs = []
    for g, win in enumerate(POOL_WINDOWS):
        c0, c1 = g * POOL_GROUP_DIM, (g + 1) * POOL_GROUP_DIM
        window_sum = (csum[:, POOL_BUF + 1:POOL_BUF + 1 + L, c0:c1]
                      - csum[:, POOL_BUF + 1 - win:POOL_BUF + 1 - win + L, c0:c1])
        count = jnp.minimum(win, pos + 1).astype(F32)[None, :, None]
        diff = window_sum / count - ext[:, POOL_BUF:, c0:c1]
        outs.append(jnp.einsum('bld,de->ble', diff, w[g].astype(F32)))
    out = jnp.concatenate(outs, axis=-1) * scale.astype(F32)
    return out.astype(a.dtype), ext[:, -POOL_BUF:].astype(a.dtype)


def sb_weights(z, qpos, kpos):
    mask = kpos[None, :] < qpos[:, None]
    log_keep = jnp.where(mask, jax.nn.log_sigmoid(-z), 0.0)
    log_after = lax.cumsum(log_keep, axis=3, reverse=True) - log_keep
    return jnp.where(mask, jnp.exp(jax.nn.log_sigmoid(z) + log_after), 0.0)


def sb_prompt(q, k, v, bias):
    B, L, H, Dh = q.shape
    kf = k.astype(F32)
    vf = v.astype(F32)
    kpos = jnp.arange(L)
    bh = bias.astype(F32)[None, :, None, None]

    def block(i):
        start = i * Q_BLOCK
        qb = lax.dynamic_slice_in_dim(q, start, Q_BLOCK, axis=1).astype(F32)
        z = jnp.einsum('bqhd,bkhd->bhqk', qb, kf) * SB_SCALE + bh
        wgt = sb_weights(z, start + jnp.arange(Q_BLOCK), kpos)
        return jnp.einsum('bhqk,bkhd->bqhd', wgt, vf)

    o = lax.map(block, jnp.arange(L // Q_BLOCK))
    return jnp.moveaxis(o, 0, 1).reshape(B, L, H, Dh).astype(q.dtype)


def sb_sample(q, k, v, k_past, v_past, pos0, bias):
    B, L, H, Dh = q.shape
    P = k_past.shape[1]
    qf = q.astype(F32)
    bh = bias.astype(F32)[None, :, None, None]
    z = jnp.concatenate([jnp.einsum('bqhd,bkhd->bhqk', qf, k_past.astype(F32)),
                         jnp.einsum('bqhd,bkhd->bhqk', qf, k.astype(F32))], axis=-1) * SB_SCALE + bh
    wgt = sb_weights(z, pos0 + jnp.arange(L), jnp.arange(P + L))
    o = (jnp.einsum('bhqk,bkhd->bqhd', wgt[..., :P], v_past.astype(F32))
         + jnp.einsum('bhqk,bkhd->bqhd', wgt[..., P:], v.astype(F32)))
    return o.astype(q.dtype)


def even_mixer(h, pos0, pool_buf, k_past, v_past, w_in, w_out, pool_w, pool_scale, sb_bias):
    B, L, _ = h.shape
    a, q, k, v = jnp.split(h @ w_in, [POOL_WIDTH, POOL_WIDTH + SB_WIDTH, POOL_WIDTH + 2 * SB_WIDTH], axis=-1)
    pool_out, new_buf = pool_mix(a, pool_buf, pos0, pool_w, pool_scale)
    q = q.reshape(B, L, SB_HEADS, SB_HEAD_DIM)
    k = k.reshape(B, L, SB_HEADS, SB_HEAD_DIM)
    v = v.reshape(B, L, SB_HEADS, SB_HEAD_DIM)
    if k_past is None:
        o = sb_prompt(q, k, v, sb_bias)
    else:
        o = sb_sample(q, k, v, k_past, v_past, pos0, sb_bias)
    out = jnp.concatenate([pool_out, o.reshape(B, L, SB_WIDTH)], axis=-1) @ w_out
    return out, new_buf, k, v


def gla_chunked(q, k, v, log_alpha, s0, chunk):
    B, L, H, DK = q.shape
    DV = v.shape[-1]
    n = L // chunk

    def split(t):
        return jnp.moveaxis(t.astype(F32).reshape(B, n, chunk, H, t.shape[-1]), 1, 0)

    causal = jnp.tril(jnp.ones((chunk, chunk), bool))

    def step(S, inp):
        qc, kc, vc, gc = inp
        b = jnp.cumsum(gc, axis=1)
        b_last = b[:, -1]
        q_dec = qc * jnp.exp(b)
        k_dec = kc * jnp.exp(-b)
        scores = jnp.where(causal, jnp.einsum('bthk,bshk->bhts', q_dec, k_dec), 0.0)
        o = (jnp.einsum('bhts,bshv->bthv', scores, vc)
             + jnp.einsum('bthk,bhkv->bthv', q_dec, S))
        k_upd = kc * jnp.exp(b_last[:, None] - b)
        S_new = jnp.exp(b_last)[..., None] * S + jnp.einsum('bshk,bshv->bhkv', k_upd, vc)
        return S_new, o

    S_fin, o = lax.scan(step, s0.astype(F32), (split(q), split(k), split(v), split(log_alpha)))
    return jnp.moveaxis(o, 0, 1).reshape(B, L, H, DV), S_fin


def gmlp_mix(u, vn, ws, bs):
    B, L, _ = u.shape
    clen = min(L, GMLP_CHUNK)
    causal = jnp.tril(jnp.ones((clen, clen), F32))
    w = ws[:, :clen, :clen].astype(F32) * causal
    vr = vn.reshape(B, L // clen, clen, GMLP_GROUPS, GMLP_GROUP_DIM).astype(F32)
    bias = jnp.swapaxes(bs[:, :clen], 0, 1).astype(F32)[None, None, :, :, None]
    mixed = jnp.einsum('gts,bcsgd->bctgd', w, vr) + bias
    return (u.astype(F32) * mixed.reshape(B, L, GMLP_WIDTH)).astype(u.dtype)


def odd_mixer(h, gla_state, w_in, w_out, w_gate, b_gate, gla_norm, gmlp_norm, ws, bs):
    B, L, _ = h.shape
    cuts = np.cumsum([GLA_KW, GLA_KW, GLA_VW, GLA_VW, GLA_RANK, GMLP_WIDTH]).tolist()
    q, k, v, r, g_low, u, vg = jnp.split(h @ w_in, cuts, axis=-1)
    q = q.reshape(B, L, GLA_HEADS, GLA_DK) * GLA_DK ** -0.5
    k = k.reshape(B, L, GLA_HEADS, GLA_DK)
    v = v.reshape(B, L, GLA_HEADS, GLA_DV)
    log_alpha = jax.nn.log_sigmoid((g_low @ w_gate + b_gate).astype(F32)) / GLA_TAU
    log_alpha = log_alpha.reshape(B, L, GLA_HEADS, GLA_DK)
    o, new_state = gla_chunked(q, k, v, log_alpha, gla_state, math.gcd(L, GLA_CHUNK))
    o = rmsnorm(o, gla_norm) * jax.nn.silu(r.reshape(B, L, GLA_HEADS, GLA_DV).astype(F32))
    vn = rmsnorm(vg, gmlp_norm)
    gm = gmlp_mix(u, vn, ws, bs)
    out = jnp.concatenate([o.reshape(B, L, GLA_VW).astype(h.dtype), gm], axis=-1) @ w_out
    return out, new_state, vn


def mem_attend(h, mk, mv, wq, wo):
    B, L, _ = h.shape
    q = (h @ wq).reshape(B, L, MEM_HEADS, MEM_HEAD_DIM).astype(F32)
    s = jnp.einsum('bqhd,bmhd->bhqm', q, mk.astype(F32)) * MEM_SCALE
    p = jax.nn.softmax(s, axis=-1)
    o = jnp.einsum('bhqm,bmhd->bqhd', p, mv.astype(F32)).reshape(B, L, MEM_WIDTH).astype(h.dtype)
    return o @ wo


def trunk(x, pos0, mem_k, mem_v, pool_state, gla_state, page_table, cache_sb_k, cache_sb_v, p):
    B, L, _ = x.shape
    sampling = page_table is not None
    pools, glas, ks, vs, gvs = [], [], [], [], []
    for l in range(DEPTH):
        h = rmsnorm(x, p['norm_mix'][l])
        if l % 2 == 0:
            e = l // 2
            if sampling:
                n_rows = page_table.shape[1] * PAGE_SIZE
                k_past = cache_sb_k[e][page_table].reshape(B, n_rows, SB_HEADS, SB_HEAD_DIM)
                v_past = cache_sb_v[e][page_table].reshape(B, n_rows, SB_HEADS, SB_HEAD_DIM)
                buf = pool_state[e]
            else:
                k_past = None
                v_past = None
                buf = jnp.zeros((B, POOL_BUF, POOL_WIDTH), x.dtype)
            out, nb, k, v = even_mixer(h, pos0, buf, k_past, v_past, p['w_in_even'][e], p['w_out_even'][e],
                                       p['pool_w'][e], p['pool_scale'][e], p['sb_bias'][e])
            pools.append(nb)
            ks.append(k)
            vs.append(v)
        else:
            o_ = l // 2
            s0 = gla_state[o_].astype(F32) if sampling else jnp.zeros((B, GLA_HEADS, GLA_DK, GLA_DV), F32)
            out, S, vn = odd_mixer(h, s0, p['w_in_odd'][o_], p['w_out_odd'][o_], p['gla_w_gate'][o_],
                                   p['gla_b_gate'][o_], p['gla_norm'][o_], p['gmlp_norm'][o_],
                                   p['gmlp_ws'][o_], p['gmlp_bs'][o_])
            glas.append(S)
            gvs.append(vn)
        x = x + out
        x = x + mem_attend(rmsnorm(x, p['norm_mem'][l]), mem_k[l], mem_v[l], p['mem_wq'][l], p['mem_wo'][l])
        f = rmsnorm(x, p['norm_ffn'][l])
        x = x + jnp.square(jax.nn.relu(f @ p['ffn_w1'][l])) @ p['ffn_w2'][l]
    y = rmsnorm(x, p['norm_final'])
    return y, jnp.stack(ks), jnp.stack(vs), jnp.stack(pools), jnp.stack(glas), jnp.stack(gvs)


def setup_inputs(seed: int = 0) -> dict:
    key = jax.random.key(seed)
    ks = iter(jax.random.split(key, 64))

    def nrm(shape, scale=1.0):
        return jax.random.normal(next(ks), shape, jnp.float32) * scale

    def gain(shape):
        return 1.0 + nrm(shape, 0.05)

    n_pages = PAST_LEN // PAGE_SIZE
    n_used = DEC_BATCH * n_pages
    n_pool = n_used + max(1, n_used // 4)
    page_table = jax.random.permutation(next(ks), n_pool)[:n_used].reshape(DEC_BATCH, n_pages).astype(jnp.int32)
    return {
        'x_prompt': nrm((BATCH, SEQ, D_MODEL)),
        'x_sample': nrm((DEC_BATCH, DEC_SEQ, D_MODEL)),
        'mem_prompt': nrm((BATCH, N_MEM, D_MODEL)),
        'cache_sb_k': nrm((N_EVEN, n_pool, PAGE_SIZE, SB_HEADS, SB_HEAD_DIM)),
        'cache_sb_v': nrm((N_EVEN, n_pool, PAGE_SIZE, SB_HEADS, SB_HEAD_DIM)),
        'page_table': page_table,
        'state_pool': nrm((N_EVEN, DEC_BATCH, POOL_BUF, POOL_WIDTH)),
        'state_gla': nrm((N_ODD, DEC_BATCH, GLA_HEADS, GLA_DK, GLA_DV)),
        'cache_mem_k': nrm((DEPTH, DEC_BATCH, N_MEM, MEM_HEADS, MEM_HEAD_DIM)),
        'cache_mem_v': nrm((DEPTH, DEC_BATCH, N_MEM, MEM_HEADS, MEM_HEAD_DIM)),
        'norm_mix': gain((DEPTH, D_MODEL)),
        'norm_mem': gain((DEPTH, D_MODEL)),
        'norm_memkv': gain((DEPTH, D_MODEL)),
        'norm_ffn': gain((DEPTH, D_MODEL)),
        'norm_final': gain((D_MODEL,)),
        'w_in_even': nrm((N_EVEN, D_MODEL, EVEN_IN), D_MODEL ** -0.5),
        'w_out_even': nrm((N_EVEN, D_MODEL, D_MODEL), (2 * D_MODEL) ** -0.5),
        'pool_w': nrm((N_EVEN, POOL_GROUPS, POOL_GROUP_DIM, POOL_GROUP_DIM), POOL_GROUP_DIM ** -0.5),
        'pool_scale': 0.5 + nrm((N_EVEN, POOL_WIDTH), 0.05),
        'sb_bias': SB_BIAS_INIT + nrm((N_EVEN, SB_HEADS), 0.1),
        'w_in_odd': nrm((N_ODD, D_MODEL, ODD_IN), D_MODEL ** -0.5),
        'w_out_odd': nrm((N_ODD, D_MODEL, D_MODEL), (2 * D_MODEL) ** -0.5),
        'gla_w_gate': nrm((N_ODD, GLA_RANK, GLA_KW), GLA_RANK ** -0.5),
        'gla_b_gate': nrm((N_ODD, GLA_KW), 0.1),
        'gla_norm': gain((N_ODD, GLA_DV)),
        'gmlp_norm': gain((N_ODD, GMLP_WIDTH)),
        'gmlp_ws': nrm((N_ODD, GMLP_GROUPS, GMLP_CHUNK, GMLP_CHUNK), GMLP_CHUNK ** -0.5),
        'gmlp_bs': 1.0 + nrm((N_ODD, GMLP_GROUPS, GMLP_CHUNK), 0.05),
        'mem_wq': nrm((DEPTH, D_MODEL, MEM_WIDTH), D_MODEL ** -0.5),
        'mem_wk': nrm((DEPTH, D_MODEL, MEM_WIDTH), D_MODEL ** -0.5),
        'mem_wv': nrm((DEPTH, D_MODEL, MEM_WIDTH), D_MODEL ** -0.5),
        'mem_wo': nrm((DEPTH, MEM_WIDTH, D_MODEL), (2 * MEM_WIDTH) ** -0.5),
        'ffn_w1': nrm((DEPTH, D_MODEL, D_FF), D_MODEL ** -0.5),
        'ffn_w2': nrm((DEPTH, D_FF, D_MODEL), (2 * D_FF) ** -0.5),
    }


def reference(x_prompt, x_sample, mem_prompt, cache_sb_k, cache_sb_v, page_table, state_pool, state_gla,
              cache_mem_k, cache_mem_v, norm_mix, norm_mem, norm_memkv, norm_ffn, norm_final,
              w_in_even, w_out_even, pool_w, pool_scale, sb_bias, w_in_odd, w_out_odd, gla_w_gate, gla_b_gate,
              gla_norm, gmlp_norm, gmlp_ws, gmlp_bs, mem_wq, mem_wk, mem_wv, mem_wo, ffn_w1, ffn_w2):
    p = {
        'norm_mix': norm_mix, 'norm_mem': norm_mem, 'norm_ffn': norm_ffn, 'norm_final': norm_final,
        'w_in_even': w_in_even, 'w_out_even': w_out_even, 'pool_w': pool_w, 'pool_scale': pool_scale,
        'sb_bias': sb_bias,
        'w_in_odd': w_in_odd, 'w_out_odd': w_out_odd, 'gla_w_gate': gla_w_gate, 'gla_b_gate': gla_b_gate,
        'gla_norm': gla_norm, 'gmlp_norm': gmlp_norm, 'gmlp_ws': gmlp_ws, 'gmlp_bs': gmlp_bs,
        'mem_wq': mem_wq, 'mem_wo': mem_wo, 'ffn_w1': ffn_w1, 'ffn_w2': ffn_w2,
    }
    B = mem_prompt.shape[0]
    mem_k_prompt = jnp.stack([(rmsnorm(mem_prompt, norm_memkv[l]) @ mem_wk[l]).reshape(B, N_MEM, MEM_HEADS, MEM_HEAD_DIM)
                              for l in range(DEPTH)])
    mem_v_prompt = jnp.stack([(rmsnorm(mem_prompt, norm_memkv[l]) @ mem_wv[l]).reshape(B, N_MEM, MEM_HEADS, MEM_HEAD_DIM)
                              for l in range(DEPTH)])
    y_prompt, sb_k_prompt, sb_v_prompt, pool_prompt, gla_prompt, _ = trunk(
        x_prompt, 0, mem_k_prompt, mem_v_prompt, None, None, None, None, None, p)
    y_sample, sb_k_sample, sb_v_sample, pool_sample, gla_sample, gmlp_v_sample = trunk(
        x_sample, PAST_LEN, cache_mem_k, cache_mem_v, state_pool, state_gla, page_table, cache_sb_k, cache_sb_v, p)
    return (y_prompt, y_sample, sb_k_prompt, sb_v_prompt, pool_prompt, gla_prompt, mem_k_prompt, mem_v_prompt,
            sb_k_sample, sb_v_sample, pool_sample, gla_sample, gmlp_v_sample)
```

```python
import functools
import math

import jax
import jax.numpy as jnp
from jax import lax
from jax.experimental import pallas as pl
from jax.experimental.pallas import tpu as pltpu

F32 = jnp.float32
BF16 = jnp.bfloat16
EPS = 1e-6

LANES = 128
PAGE_SIZE = 128
POOL_WINDOWS = (2, 4, 8, 16)
POOL_HALO = 16
POOL_GROUP_DIM = 128
SB_HEAD_DIM = 128
SB_BLOCK = 256
SB_COLS_PER_HEAD = 8
GLA_HEADS = 4
GLA_DK = 128
GLA_DV = 256
GLA_RANK = 16
GLA_TAU = 16.0
GLA_CHUNK = 32
GMLP_GROUPS = 4
GMLP_GROUP_DIM = 256
GMLP_CHUNK = 128
MEM_HEADS = 4
MEM_HEAD_DIM = 128
SAMPLE_ROWS = 16
VMEM_LIMIT = 56 * 1024 * 1024


def _params(n_axes):
    return pltpu.CompilerParams(
        dimension_semantics=("arbitrary",) * n_axes, vmem_limit_bytes=VMEM_LIMIT)


def _tile(n, candidates):
    for c in candidates:
        if n % c == 0:
            return c
    return n


def _softplus(z):
    return jnp.maximum(z, 0.0) + jnp.log1p(jnp.exp(-jnp.abs(z)))


def _split_bf16(x):
    hi = x.astype(BF16)
    lo = (x - hi.astype(F32)).astype(BF16)
    return hi, lo


def _nmm_kernel(x_ref, g_ref, w_ref, o_ref, xn_ref, *, act, rows):
    @pl.when(pl.program_id(1) == 0)
    def _():
        g = g_ref[...]

        def body(c, carry):
            r0 = pl.multiple_of(c * rows, rows)
            x = x_ref[pl.ds(r0, rows), :]
            ms = jnp.mean(x * x, axis=-1, keepdims=True)
            xn_ref[pl.ds(r0, rows), :] = (x * lax.rsqrt(ms + EPS) * g).astype(BF16)
            return carry

        lax.fori_loop(0, x_ref.shape[0] // rows, body, 0)

    y = jnp.dot(xn_ref[...], w_ref[...], preferred_element_type=F32)
    if act == "relu2":
        y = jnp.square(jnp.maximum(y, 0.0))
    o_ref[...] = y.astype(o_ref.dtype)


def _nmm(x, g, w, *, act=None, out_dtype=F32):
    M, K = x.shape
    N = w.shape[1]
    tm = _tile(M, (1024, 512, 256))
    tn = _tile(N, (1024, 768, 512, 256, 128))
    rows = _tile(tm, (128, 64, 32, 16, 8))
    return pl.pallas_call(
        functools.partial(_nmm_kernel, act=act, rows=rows),
        grid=(M // tm, N // tn),
        in_specs=[
            pl.BlockSpec((tm, K), lambda i, j: (i, 0)),
            pl.BlockSpec((1, K), lambda i, j: (0, 0)),
            pl.BlockSpec((K, tn), lambda i, j: (0, j)),
        ],
        out_specs=pl.BlockSpec((tm, tn), lambda i, j: (i, j)),
        out_shape=jax.ShapeDtypeStruct((M, N), out_dtype),
        scratch_shapes=[pltpu.VMEM((tm, K), BF16)],
        compiler_params=_params(2),
        name="norm_matmul",
    )(x, g.reshape(1, K), w)


def _mmres_kernel(*refs, n1, nk, two):
    if two:
        a1_ref, a2_ref, w_ref, res_ref, o_ref = refs
    else:
        a1_ref, w_ref, res_ref, o_ref = refs
        a2_ref = None
    k = pl.program_id(2)

    @pl.when(k == 0)
    def _():
        o_ref[...] = res_ref[...]

    @pl.when(k < n1)
    def _():
        o_ref[...] += jnp.dot(a1_ref[...], w_ref[...], preferred_element_type=F32)

    if two:
        @pl.when(k >= n1)
        def _():
            o_ref[...] += jnp.dot(a2_ref[...], w_ref[...], preferred_element_type=F32)


def _mm_res(a1, a2, w, res):
    M, K1 = a1.shape
    K2 = 0 if a2 is None else a2.shape[1]
    N = w.shape[1]
    tm = _tile(M, (1024, 512, 256))
    tn = _tile(N, (1024, 512, 256))
    tk = _tile(math.gcd(K1, K2) if K2 else K1, (1024, 512, 256, 128))
    n1 = K1 // tk
    nk = (K1 + K2) // tk
    two = a2 is not None
    in_specs = [pl.BlockSpec((tm, tk), lambda i, j, k: (i, jnp.minimum(k, n1 - 1)))]
    args = [a1]
    if two:
        in_specs.append(pl.BlockSpec((tm, tk), lambda i, j, k: (i, jnp.maximum(k - n1, 0))))
        args.append(a2)
    in_specs += [
        pl.BlockSpec((tk, tn), lambda i, j, k: (k, j)),
        pl.BlockSpec((tm, tn), lambda i, j, k: (i, j)),
    ]
    args += [w, res]
    return pl.pallas_call(
        functools.partial(_mmres_kernel, n1=n1, nk=nk, two=two),
        grid=(M // tm, N // tn, nk),
        in_specs=in_specs,
        out_specs=pl.BlockSpec((tm, tn), lambda i, j, k: (i, j)),
        out_shape=jax.ShapeDtypeStruct((M, N), F32),
        compiler_params=_params(3),
        name="matmul_residual",
    )(*args)


def _rms_kernel(x_ref, g_ref, o_ref):
    x = x_ref[...]
    ms = jnp.mean(x * x, axis=-1, keepdims=True)
    o_ref[...] = x * lax.rsqrt(ms + EPS) * g_ref[...]


def _rmsnorm(x, g):
    M, K = x.shape
    tm = _tile(M, (256, 128, 64, 8))
    return pl.pallas_call(
        _rms_kernel,
        grid=(M // tm,),
        in_specs=[pl.BlockSpec((tm, K), lambda i: (i, 0)), pl.BlockSpec((1, K), lambda i: (0, 0))],
        out_specs=pl.BlockSpec((tm, K), lambda i: (i, 0)),
        out_shape=jax.ShapeDtypeStruct((M, K), F32),
        compiler_params=_params(1),
        name="final_rmsnorm",
    )(x, g.reshape(1, K))


def _pool_kernel(ext_ref, w_ref, scale_ref, o_ref, diff_ref, *, bb, lc, pos_base, pos_stride):
    blk = pl.program_id(0)
    g = pl.program_id(1)
    win = jnp.where(g == 0, POOL_WINDOWS[0],
                    jnp.where(g == 1, POOL_WINDOWS[1],
                              jnp.where(g == 2, POOL_WINDOWS[2], POOL_WINDOWS[3])))
    t_iota = lax.broadcasted_iota(jnp.int32, (lc, LANES), 0)

    def body(b, carry):
        def e(i):
            return ext_ref[b, pl.ds(POOL_HALO - i, lc), :]

        cur = e(0)
        s2 = cur + e(1)
        s4 = s2 + e(2) + e(3)
        s8 = s4 + e(4) + e(5) + e(6) + e(7)
        s16 = s8 + e(8) + e(9) + e(10) + e(11) + e(12) + e(13) + e(14) + e(15)
        ws = jnp.where(g == 0, s2, jnp.where(g == 1, s4, jnp.where(g == 2, s8, s16)))
        pos = pos_base + (blk * bb + b) * pos_stride + t_iota
        cnt = jnp.minimum(win, pos + 1).astype(F32)
        r0 = pl.multiple_of(b * lc, lc)
        diff_ref[pl.ds(r0, lc), :] = ws / cnt - cur
        return carry

    lax.fori_loop(0, bb, body, 0)
    out = jnp.dot(diff_ref[...].astype(BF16), w_ref[0].astype(BF16), preferred_element_type=F32)
    o_ref[...] = (out * scale_ref[...]).astype(o_ref.dtype)


def _pool_mix(ext, w, scale, *, pos_base, pos_stride):
    NB, rows, width = ext.shape
    lc = rows - POOL_HALO
    bb = NB if NB * lc <= 2048 else _tile(NB, (4, 2, 1))
    G = width // POOL_GROUP_DIM
    return pl.pallas_call(
        functools.partial(_pool_kernel, bb=bb, lc=lc, pos_base=pos_base, pos_stride=pos_stride),
        grid=(NB // bb, G),
        in_specs=[
            pl.BlockSpec((bb, rows, POOL_GROUP_DIM), lambda i, g: (i, 0, g)),
            pl.BlockSpec((1, POOL_GROUP_DIM, POOL_GROUP_DIM), lambda i, g: (g, 0, 0)),
            pl.BlockSpec((1, POOL_GROUP_DIM), lambda i, g: (0, g)),
        ],
        out_specs=pl.BlockSpec((bb * lc, POOL_GROUP_DIM), lambda i, g: (i, g)),
        out_shape=jax.ShapeDtypeStruct((NB * lc, width), BF16),
        scratch_shapes=[pltpu.VMEM((bb * lc, POOL_GROUP_DIM), F32)],
        compiler_params=_params(2),
        name="pool_mix",
    )(ext, w, scale.reshape(1, width))


def _suffix_matrix(n):
    r = jnp.arange(n)[:, None]
    c = jnp.arange(n)[None, :]
    u = (r > c).astype(BF16)
    uj = jnp.concatenate([u, jnp.ones((n, n), BF16)], axis=1)
    return jnp.concatenate([uj, uj], axis=0)


def _sb_prompt_kernel(bias_ref, q_ref, k_ref, v_ref, uj_ref, o_ref, acc_ref, run_ref, *, scale):
    h = pl.program_id(0)
    i = pl.program_id(1)
    tq = SB_BLOCK
    half = SB_BLOCK // 2
    bias = bias_ref[h]
    q = q_ref[...].astype(BF16)
    uj = uj_ref[...]
    qpos = i * tq + lax.broadcasted_iota(jnp.int32, (tq, tq), 0)
    col = lax.broadcasted_iota(jnp.int32, (tq, tq), 1)
    acc_ref[...] = jnp.zeros_like(acc_ref)
    run_ref[...] = jnp.zeros_like(run_ref)

    def suffix(lk_half):
        hi, lo = _split_bf16(lk_half)
        r = jnp.dot(jnp.concatenate([hi, lo], axis=1), uj, preferred_element_type=F32)
        return r[:, :half], r[:, half:]

    def body(jj, carry):
        start = pl.multiple_of((i - jj) * tq, tq)
        kb = k_ref[pl.ds(start, tq), :].astype(BF16)
        vb = v_ref[pl.ds(start, tq), :].astype(BF16)
        z = lax.dot_general(q, kb, (((1,), (1,)), ((), ())), preferred_element_type=F32)
        z = z * scale + bias
        mask = (start + col) < qpos
        sp = _softplus(z)
        lk = jnp.where(mask, -sp, 0.0)
        la0, t0 = suffix(lk[:, :half])
        la1, t1 = suffix(lk[:, half:])
        run = run_ref[...]
        la = jnp.concatenate([la0 + (t1 + run), la1 + run], axis=1)
        wgt = jnp.where(mask, jnp.exp(z - sp + la), 0.0).astype(BF16)
        acc_ref[...] += jnp.dot(wgt, vb, preferred_element_type=F32)
        run_ref[...] = run + t0 + t1
        return carry

    lax.fori_loop(0, i + 1, body, 0)
    o_ref[...] = acc_ref[...].astype(o_ref.dtype)


def _sb_prompt(p, bias, *, q_col, k_col, v_col, heads):
    L = p.shape[0]
    tq = SB_BLOCK
    scale = SB_HEAD_DIM ** -0.5
    return pl.pallas_call(
        functools.partial(_sb_prompt_kernel, scale=scale),
        grid=(heads, L // tq),
        in_specs=[
            pl.BlockSpec(memory_space=pltpu.SMEM),
            pl.BlockSpec((tq, SB_HEAD_DIM), lambda h, i: (i, q_col + h)),
            pl.BlockSpec((L, SB_HEAD_DIM), lambda h, i: (0, k_col + h)),
            pl.BlockSpec((L, SB_HEAD_DIM), lambda h, i: (0, v_col + h)),
            pl.BlockSpec((tq, tq), lambda h, i: (0, 0)),
        ],
        out_specs=pl.BlockSpec((tq, SB_HEAD_DIM), lambda h, i: (i, h)),
        out_shape=jax.ShapeDtypeStruct((L, heads * SB_HEAD_DIM), BF16),
        scratch_shapes=[pltpu.VMEM((tq, SB_HEAD_DIM), F32), pltpu.VMEM((tq, SB_BLOCK // 2), F32)],
        compiler_params=_params(2),
        name="sb_prompt",
    )(bias, p, p, p, _suffix_matrix(SB_BLOCK // 2))


def _sb_sample_kernel(pt_ref, qbd_ref, knew_ref, vnew_ref, *rest, pps, n_new, scale):
    k_refs = rest[:pps]
    v_refs = rest[pps:2 * pps]
    bias_ref, u_ref, o_ref, acc_ref, run_ref, kpad_ref, vpad_ref = rest[2 * pps:]
    s = pl.program_id(1)
    qbd = qbd_ref[0]
    bias = bias_ref[...]
    u2 = u_ref[...]

    def process(kblk, vblk, mask):
        zt = jnp.dot(kblk.astype(BF16), qbd, preferred_element_type=F32) * scale + bias
        sp = _softplus(zt)
        lk = -sp if mask is None else jnp.where(mask, -sp, 0.0)
        hi, lo = _split_bf16(lk)
        la = jnp.dot(u2, jnp.concatenate([hi, lo], axis=0), preferred_element_type=F32)
        run = run_ref[0:1, :]
        wt = jnp.exp(zt - sp + la + run)
        if mask is not None:
            wt = jnp.where(mask, wt, 0.0)
        w = wt.T.astype(BF16)
        acc_ref[...] += jnp.dot(w, vblk.astype(BF16), preferred_element_type=F32)
        run_ref[0:1, :] = run + jnp.sum(lk, axis=0, keepdims=True)

    @pl.when(s == 0)
    def _():
        acc_ref[...] = jnp.zeros_like(acc_ref)
        run_ref[...] = jnp.zeros_like(run_ref)
        kpad_ref[...] = jnp.zeros_like(kpad_ref)
        vpad_ref[...] = jnp.zeros_like(vpad_ref)
        kpad_ref[0:SB_COLS_PER_HEAD, :] = knew_ref[0]
        vpad_ref[0:SB_COLS_PER_HEAD, :] = vnew_ref[0]
        krow = lax.broadcasted_iota(jnp.int32, (PAGE_SIZE, LANES), 0)
        t = lax.broadcasted_iota(jnp.int32, (PAGE_SIZE, LANES), 1) % SB_COLS_PER_HEAD
        mask = jnp.logical_and(krow < t, krow < n_new)
        process(kpad_ref[...], vpad_ref[...], mask)

    for r in range(pps):
        process(k_refs[r][0], v_refs[r][0], None)

    @pl.when(s == pl.num_programs(1) - 1)
    def _():
        heads = o_ref.shape[2] // SB_HEAD_DIM
        for h in range(heads):
            o_ref[0, :, h * SB_HEAD_DIM:(h + 1) * SB_HEAD_DIM] = acc_ref[
                h * SB_COLS_PER_HEAD:(h + 1) * SB_COLS_PER_HEAD,
                h * SB_HEAD_DIM:(h + 1) * SB_HEAD_DIM].astype(o_ref.dtype)


def _sb_sample(q, knew, vnew, cache_k, cache_v, page_table, bias):
    B, L, H, Dh = q.shape
    W = H * Dh
    NP = page_table.shape[1]
    pps = math.gcd(NP, 4)
    c = SB_COLS_PER_HEAD
    qt = jnp.pad(jnp.transpose(q, (0, 2, 3, 1)), ((0, 0), (0, 0), (0, 0), (0, c - L)))
    eye = jnp.eye(H, LANES // c, dtype=bool)[None, :, None, :, None]
    qbd = jnp.where(eye, qt[:, :, :, None, :], 0.0).reshape(B, W, LANES).astype(BF16)
    bias_row = jnp.pad(jnp.repeat(bias.astype(F32), c), (0, LANES - H * c)).reshape(1, LANES)
    kn = jnp.pad(knew.reshape(B, L, W), ((0, 0), (0, c - L), (0, 0)))
    vn = jnp.pad(vnew.reshape(B, L, W), ((0, 0), (0, c - L), (0, 0)))
    r = jnp.arange(PAGE_SIZE)
    ut = (r[None, :] > r[:, None]).astype(BF16)
    u2 = jnp.concatenate([ut, ut], axis=1)

    def page_spec(rr):
        return pl.BlockSpec((1, PAGE_SIZE, W),
                            lambda b, s, pt: (pt[b, NP - 1 - (s * pps + rr)], 0, 0))

    seq3 = lambda b, s, pt: (b, 0, 0)
    grid_spec = pltpu.PrefetchScalarGridSpec(
        num_scalar_prefetch=1,
        grid=(B, NP // pps),
        in_specs=[
            pl.BlockSpec((1, W, LANES), seq3),
            pl.BlockSpec((1, c, W), seq3),
            pl.BlockSpec((1, c, W), seq3),
            *[page_spec(rr) for rr in range(pps)],
            *[page_spec(rr) for rr in range(pps)],
            pl.BlockSpec((1, LANES), lambda b, s, pt: (0, 0)),
            pl.BlockSpec((PAGE_SIZE, 2 * PAGE_SIZE), lambda b, s, pt: (0, 0)),
        ],
        out_specs=pl.BlockSpec((1, c, W), seq3),
        scratch_shapes=[
            pltpu.VMEM((LANES, W), F32),
            pltpu.VMEM((8, LANES), F32),
            pltpu.VMEM((PAGE_SIZE, W), F32),
            pltpu.VMEM((PAGE_SIZE, W), F32),
        ],
    )
    out = pl.pallas_call(
        functools.partial(_sb_sample_kernel, pps=pps, n_new=L, scale=Dh ** -0.5),
        grid_spec=grid_spec,
        out_shape=jax.ShapeDtypeStruct((B, c, W), BF16),
        compiler_params=_params(2),
        name="sb_sample",
    )(page_table, qbd, kn, vn, *([cache_k] * pps), *([cache_v] * pps), bias_row, u2)
    return out[:, :L].reshape(B * L, W)


def _mem_kernel(q_ref, k_ref, v_ref, o_ref, *, scale):
    for h in range(MEM_HEADS):
        sl = slice(h * MEM_HEAD_DIM, (h + 1) * MEM_HEAD_DIM)
        q = q_ref[0, :, sl]
        k = k_ref[0, :, sl].astype(BF16)
        v = v_ref[0, :, sl].astype(BF16)
        s = lax.dot_general(q, k, (((1,), (1,)), ((), ())), preferred_element_type=F32) * scale
        m = jnp.max(s, axis=-1, keepdims=True)
        e = jnp.exp(s - m)
        p = e / jnp.sum(e, axis=-1, keepdims=True)
        o_ref[0, :, sl] = jnp.dot(p.astype(BF16), v, preferred_element_type=F32).astype(o_ref.dtype)


def _mem_attend(q, mk, mv):
    B, L, W = q.shape
    n_mem = mk.shape[1]
    tq = _tile(L, (512, 256, 128, 64, 32, 16))
    return pl.pallas_call(
        functools.partial(_mem_kernel, scale=MEM_HEAD_DIM ** -0.5),
        grid=(B, L // tq),
        in_specs=[
            pl.BlockSpec((1, tq, W), lambda b, i: (b, i, 0)),
            pl.BlockSpec((1, n_mem, W), lambda b, i: (b, 0, 0)),
            pl.BlockSpec((1, n_mem, W), lambda b, i: (b, 0, 0)),
        ],
        out_specs=pl.BlockSpec((1, tq, W), lambda b, i: (b, i, 0)),
        out_shape=jax.ShapeDtypeStruct((B, L, W), BF16),
        compiler_params=_params(2),
        name="mem_attend",
    )(q, mk, mv)


def _gla_kernel(q_ref, k_ref, v_ref, r_ref, gl_ref, wg_ref, bg_ref, gn_ref, lt_ref, s0_ref,
                o_ref, sout_ref, s_ref, *, chunk, n_valid, n_chunks):
    i = pl.program_id(1)
    tm = chunk * n_chunks

    @pl.when(i == 0)
    def _():
        s_ref[...] = s0_ref[0]

    x = jnp.dot(gl_ref[0].astype(BF16), wg_ref[...].astype(BF16),
                preferred_element_type=F32) + bg_ref[...]
    la = -_softplus(-x) / GLA_TAU
    if n_valid < chunk:
        row = lax.broadcasted_iota(jnp.int32, la.shape, 0) % chunk
        la = jnp.where(row < n_valid, la, 0.0)
    hi, lo = _split_bf16(la)
    bcum = jnp.dot(lt_ref[...], jnp.concatenate([hi, lo], axis=0), preferred_element_type=F32)
    tri = (lax.broadcasted_iota(jnp.int32, (chunk, chunk), 0)
           >= lax.broadcasted_iota(jnp.int32, (chunk, chunk), 1))
    gn = gn_ref[...]
    qscale = GLA_DK ** -0.5
    for c in range(n_chunks):
        rows = slice(c * chunk, (c + 1) * chunk)
        for h in range(GLA_HEADS):
            kcols = slice(h * GLA_DK, (h + 1) * GLA_DK)
            vcols = slice(h * GLA_DV, (h + 1) * GLA_DV)
            b = bcum[rows, kcols]
            bl = b[chunk - 1:chunk, :]
            qh = q_ref[0, rows, kcols] * qscale
            kh = k_ref[0, rows, kcols]
            vh = v_ref[0, rows, vcols].astype(BF16)
            qd = (qh * jnp.exp(b)).astype(BF16)
            kd = (kh * jnp.exp(-b)).astype(BF16)
            sc = lax.dot_general(qd, kd, (((1,), (1,)), ((), ())), preferred_element_type=F32)
            sc = jnp.where(tri, sc, 0.0).astype(BF16)
            st = s_ref[h]
            o = (jnp.dot(sc, vh, preferred_element_type=F32)
                 + jnp.dot(qd, st.astype(BF16), preferred_element_type=F32))
            ku = (kh * jnp.exp(bl - b)).astype(BF16)
            upd = lax.dot_general(ku, vh, (((0,), (0,)), ((), ())), preferred_element_type=F32)
            decay = jnp.exp(jnp.broadcast_to(bl, (GLA_DK, GLA_DK)).T)
            s_ref[h] = jnp.concatenate([decay, decay], axis=1) * st + upd
            ms = jnp.mean(o * o, axis=-1, keepdims=True)
            rh = r_ref[0, rows, vcols]
            gate = rh / (1.0 + jnp.exp(-rh))
            o_ref[0, rows, vcols] = (o * lax.rsqrt(ms + EPS) * gn * gate).astype(o_ref.dtype)

    @pl.when(i == pl.num_programs(1) - 1)
    def _():
        sout_ref[0] = s_ref[...]


def _gla(p, s0, w_gate, b_gate, gnorm, *, chunk, n_valid, gl_col):
    B, Lp, W = p.shape
    n_chunks = _tile(Lp // chunk, (4, 2, 1))
    tm = chunk * n_chunks
    kw = GLA_HEADS * GLA_DK
    vw = GLA_HEADS * GLA_DV
    glw = 256
    wg = jnp.pad(w_gate, ((0, glw - w_gate.shape[0]), (0, 0)))
    r = jnp.arange(tm)
    lt = ((r[:, None] >= r[None, :]) & (r[:, None] // chunk == r[None, :] // chunk)).astype(BF16)
    lt2 = jnp.concatenate([lt, lt], axis=1)
    o, s_out = pl.pallas_call(
        functools.partial(_gla_kernel, chunk=chunk, n_valid=n_valid, n_chunks=n_chunks),
        grid=(B, Lp // tm),
        in_specs=[
            pl.BlockSpec((1, tm, kw), lambda b, i: (b, i, 0)),
            pl.BlockSpec((1, tm, kw), lambda b, i: (b, i, 1)),
            pl.BlockSpec((1, tm, vw), lambda b, i: (b, i, 1)),
            pl.BlockSpec((1, tm, vw), lambda b, i: (b, i, 2)),
            pl.BlockSpec((1, tm, glw), lambda b, i: (b, i, gl_col // glw)),
            pl.BlockSpec((glw, kw), lambda b, i: (0, 0)),
            pl.BlockSpec((1, kw), lambda b, i: (0, 0)),
            pl.BlockSpec((1, GLA_DV), lambda b, i: (0, 0)),
            pl.BlockSpec((tm, 2 * tm), lambda b, i: (0, 0)),
            pl.BlockSpec((1, GLA_HEADS, GLA_DK, GLA_DV), lambda b, i: (b, 0, 0, 0)),
        ],
        out_specs=[
            pl.BlockSpec((1, tm, vw), lambda b, i: (b, i, 0)),
            pl.BlockSpec((1, GLA_HEADS, GLA_DK, GLA_DV), lambda b, i: (b, 0, 0, 0)),
        ],
        out_shape=[
            jax.ShapeDtypeStruct((B, Lp, vw), BF16),
            jax.ShapeDtypeStruct((B, GLA_HEADS, GLA_DK, GLA_DV), F32),
        ],
        scratch_shapes=[pltpu.VMEM((GLA_HEADS, GLA_DK, GLA_DV), F32)],
        compiler_params=_params(2),
        name="gla",
    )(p, p, p, p, p, wg, b_gate.reshape(1, kw), gnorm.reshape(1, GLA_DV), lt2, s0)
    return o, s_out


def _gmlp_kernel(u_ref, vg_ref, gn_ref, w_ref, b_ref, o_ref, vn_ref):
    vg = vg_ref[...]
    ms = jnp.mean(vg * vg, axis=-1, keepdims=True)
    vn = vg * lax.rsqrt(ms + EPS) * gn_ref[...]
    vn_ref[...] = vn
    n = w_ref.shape[1]
    tri = (lax.broadcasted_iota(jnp.int32, (n, n), 0) >= lax.broadcasted_iota(jnp.int32, (n, n), 1))
    for g in range(GMLP_GROUPS):
        cols = slice(g * GMLP_GROUP_DIM, (g + 1) * GMLP_GROUP_DIM)
        w = jnp.where(tri, w_ref[g], 0.0).astype(BF16)
        mixed = jnp.dot(w, vn[:, cols].astype(BF16), preferred_element_type=F32) + b_ref[:, cols]
        o_ref[:, cols] = (u_ref[:, cols] * mixed).astype(o_ref.dtype)


def _gmlp(p, gnorm, w, bias_full, *, u_col, vg_col):
    M = p.shape[0]
    n = GMLP_CHUNK
    width = GMLP_GROUPS * GMLP_GROUP_DIM
    return pl.pallas_call(
        _gmlp_kernel,
        grid=(M // n,),
        in_specs=[
            pl.BlockSpec((n, width), lambda i: (i, u_col)),
            pl.BlockSpec((n, width), lambda i: (i, vg_col)),
            pl.BlockSpec((1, width), lambda i: (0, 0)),
            pl.BlockSpec((GMLP_GROUPS, n, n), lambda i: (0, 0, 0)),
            pl.BlockSpec((n, width), lambda i: (0, 0)),
        ],
        out_specs=[pl.BlockSpec((n, width), lambda i: (i, 0)), pl.BlockSpec((n, width), lambda i: (i, 0))],
        out_shape=[jax.ShapeDtypeStruct((M, width), BF16), jax.ShapeDtypeStruct((M, width), F32)],
        compiler_params=_params(1),
        name="gmlp",
    )(p, p, gnorm.reshape(1, width), w, bias_full)


def _pad_rows(a, rows):
    return jnp.pad(a, ((0, 0), (0, rows - a.shape[1]), (0, 0)))


def _trunk(x, B, L, pos0, mem_k, mem_v, pool_state, gla_state, page_table, cache_sb_k, cache_sb_v, wts):
    sampling = page_table is not None
    M, D = x.shape
    depth = wts["norm_mix"].shape[0]
    pools, glas, ks, vs, gvs = [], [], [], [], []
    for l in range(depth):
        if l % 2 == 0:
            e = l // 2
            pool_w = wts["pool_w"][e]
            pw = pool_w.shape[0] * pool_w.shape[1]
            sbw = (wts["w_in_even"][e].shape[1] - pw) // 3
            heads = sbw // SB_HEAD_DIM
            p = _nmm(x, wts["norm_mix"][l], wts["w_in_even"][e])
            a = p[:, :pw].reshape(B, L, pw)
            k = p[:, pw + sbw:pw + 2 * sbw].reshape(B, L, heads, SB_HEAD_DIM)
            v = p[:, pw + 2 * sbw:].reshape(B, L, heads, SB_HEAD_DIM)
            if sampling:
                buf = pool_state[e]
                ext = jnp.concatenate(
                    [jnp.zeros((B, POOL_HALO - buf.shape[1], pw), F32), buf, _pad_rows(a, 8)], axis=1)
                pool_out = _pool_mix(ext, pool_w, wts["pool_scale"][e], pos_base=pos0, pos_stride=0)
                pool_out = pool_out.reshape(B, 8, pw)[:, :L].reshape(M, pw)
                new_buf = jnp.concatenate([buf, a], axis=1)[:, -buf.shape[1]:]
                q = p[:, pw:pw + sbw].reshape(B, L, heads, SB_HEAD_DIM)
                n_pool = cache_sb_k.shape[1]
                o = _sb_sample(q, k, v, cache_sb_k[e].reshape(n_pool, PAGE_SIZE, sbw),
                               cache_sb_v[e].reshape(n_pool, PAGE_SIZE, sbw), page_table,
                               wts["sb_bias"][e])
            else:
                lc = _tile(L, (256, 128, 64, 32, 16, 8))
                nb = L // lc
                full = jnp.concatenate([jnp.zeros((POOL_HALO, pw), F32), a[0]], axis=0)
                ext = jnp.stack([full[c * lc:c * lc + lc + POOL_HALO] for c in range(nb)])
                pool_out = _pool_mix(ext, pool_w, wts["pool_scale"][e], pos_base=0, pos_stride=lc)
                hist = POOL_HALO - 1
                new_buf = jnp.concatenate([jnp.zeros((B, hist, pw), F32), a], axis=1)[:, -hist:]
                o = _sb_prompt(p, wts["sb_bias"][e].astype(F32), q_col=pw // SB_HEAD_DIM,
                               k_col=(pw + sbw) // SB_HEAD_DIM, v_col=(pw + 2 * sbw) // SB_HEAD_DIM,
                               heads=heads)
            x = _mm_res(pool_out, o, wts["w_out_even"][e], x)
            pools.append(new_buf)
            ks.append(k)
            vs.append(v)
        else:
            o_ = l // 2
            p = _nmm(x, wts["norm_mix"][l], wts["w_in_odd"][o_])
            W = p.shape[1]
            gl_col = 2 * GLA_HEADS * GLA_DK + 2 * GLA_HEADS * GLA_DV + 2 * GMLP_GROUPS * GMLP_GROUP_DIM
            ws = wts["gmlp_ws"][o_]
            bs = wts["gmlp_bs"][o_]
            if sampling:
                rows = SAMPLE_ROWS
                pp = _pad_rows(p.reshape(B, L, W), rows)
                og, s_new = _gla(pp, gla_state[o_], wts["gla_w_gate"][o_], wts["gla_b_gate"][o_],
                                 wts["gla_norm"][o_], chunk=rows, n_valid=L, gl_col=gl_col)
                og = og[:, :L].reshape(M, -1)
                per = GMLP_CHUNK // rows
                w_small = jnp.pad(ws[:, :L, :L], ((0, 0), (0, rows - L), (0, rows - L)))
                sel = jnp.eye(per, dtype=bool)[None, :, None, :, None]
                w_bd = jnp.where(sel, w_small[:, None, :, None, :], 0.0).reshape(
                    GMLP_GROUPS, GMLP_CHUNK, GMLP_CHUNK)
                b_small = jnp.tile(jnp.pad(bs[:, :L], ((0, 0), (0, rows - L))), (1, per))
                bias_full = jnp.repeat(b_small.T, GMLP_GROUP_DIM, axis=1)
                gm, vn = _gmlp(pp.reshape(B * rows, W), wts["gmlp_norm"][o_], w_bd, bias_full,
                               u_col=3, vg_col=4)
                gm = gm.reshape(B, rows, -1)[:, :L].reshape(M, -1)
                vn = vn.reshape(B, rows, -1)[:, :L]
            else:
                chunk = math.gcd(L, GLA_CHUNK)
                s0 = jnp.zeros((B, GLA_HEADS, GLA_DK, GLA_DV), F32)
                og, s_new = _gla(p.reshape(B, L, W), s0, wts["gla_w_gate"][o_], wts["gla_b_gate"][o_],
                                 wts["gla_norm"][o_], chunk=chunk, n_valid=chunk, gl_col=gl_col)
                og = og.reshape(M, -1)
                bias_full = jnp.repeat(bs.T, GMLP_GROUP_DIM, axis=1)
                gm, vn = _gmlp(p, wts["gmlp_norm"][o_], ws, bias_full, u_col=3, vg_col=4)
                vn = vn.reshape(B, L, -1)
            x = _mm_res(og, gm, wts["w_out_odd"][o_], x)
            glas.append(s_new)
            gvs.append(vn)
        qm = _nmm(x, wts["norm_mem"][l], wts["mem_wq"][l], out_dtype=BF16)
        mw = qm.shape[1]
        if sampling:
            qm = _pad_rows(qm.reshape(B, L, mw), SAMPLE_ROWS)
            om = _mem_attend(qm, mem_k[l], mem_v[l])[:, :L].reshape(M, mw)
        else:
            om = _mem_attend(qm.reshape(B, L, mw), mem_k[l], mem_v[l]).reshape(M, mw)
        x = _mm_res(om, None, wts["mem_wo"][l], x)
        hdn = _nmm(x, wts["norm_ffn"][l], wts["ffn_w1"][l], act="relu2", out_dtype=BF16)
        x = _mm_res(hdn, None, wts["ffn_w2"][l], x)
    y = _rmsnorm(x, wts["norm_final"])
    return y, jnp.stack(ks), jnp.stack(vs), jnp.stack(pools), jnp.stack(glas), jnp.stack(gvs)


def _prep_w_in_odd(w):
    main = 2 * GLA_HEADS * GLA_DK + 2 * GLA_HEADS * GLA_DV
    gate = w[:, :, main:main + GLA_RANK]
    rest = w[:, :, main + GLA_RANK:]
    gate = jnp.pad(gate, ((0, 0), (0, 0), (0, 256 - GLA_RANK)))
    return jnp.concatenate([w[:, :, :main], rest, gate], axis=-1)


def kernel(x_prompt, x_sample, mem_prompt, cache_sb_k, cache_sb_v, page_table, state_pool, state_gla, cache_mem_k, cache_mem_v, norm_mix, norm_mem, norm_memkv, norm_ffn, norm_final, w_in_even, w_out_even, pool_w, pool_scale, sb_bias, w_in_odd, w_out_odd, gla_w_gate, gla_b_gate, gla_norm, gmlp_norm, gmlp_ws, gmlp_bs, mem_wq, mem_wk, mem_wv, mem_wo, ffn_w1, ffn_w2):
    bf = lambda w: w.astype(BF16)
    wts = {
        "norm_mix": norm_mix, "norm_mem": norm_mem, "norm_ffn": norm_ffn, "norm_final": norm_final,
        "w_in_even": bf(w_in_even), "w_out_even": bf(w_out_even), "pool_w": pool_w,
        "pool_scale": pool_scale, "sb_bias": sb_bias,
        "w_in_odd": bf(_prep_w_in_odd(w_in_odd)), "w_out_odd": bf(w_out_odd),
        "gla_w_gate": gla_w_gate, "gla_b_gate": gla_b_gate, "gla_norm": gla_norm,
        "gmlp_norm": gmlp_norm, "gmlp_ws": gmlp_ws, "gmlp_bs": gmlp_bs,
        "mem_wq": bf(mem_wq), "mem_wo": bf(mem_wo), "ffn_w1": bf(ffn_w1), "ffn_w2": bf(ffn_w2),
    }
    depth = norm_mix.shape[0]
    B, L, D = x_prompt.shape
    n_mem = mem_prompt.shape[1]
    mp = mem_prompt.reshape(B * n_mem, D)
    mkp = jnp.stack([_nmm(mp, norm_memkv[l], bf(mem_wk[l])) for l in range(depth)])
    mvp = jnp.stack([_nmm(mp, norm_memkv[l], bf(mem_wv[l])) for l in range(depth)])
    mw = mkp.shape[-1]
    mkp3 = mkp.reshape(depth, B, n_mem, mw)
    mvp3 = mvp.reshape(depth, B, n_mem, mw)
    y_p, sbk_p, sbv_p, pool_p, gla_p, _ = _trunk(
        x_prompt.reshape(B * L, D), B, L, 0, mkp3, mvp3, None, None, None, None, None, wts)
    Bs, Ls, _ = x_sample.shape
    past_len = page_table.shape[1] * PAGE_SIZE
    cmk = cache_mem_k.reshape(depth, Bs, n_mem, mw)
    cmv = cache_mem_v.reshape(depth, Bs, n_mem, mw)
    y_s, sbk_s, sbv_s, pool_s, gla_s, gv_s = _trunk(
        x_sample.reshape(Bs * Ls, D), Bs, Ls, past_len, cmk, cmv, state_pool, state_gla,
        page_table, cache_sb_k, cache_sb_v, wts)
    mem_shape = (depth, B, n_mem, MEM_HEADS, MEM_HEAD_DIM)
    return (y_p.reshape(B, L, D), y_s.reshape(Bs, Ls, D), sbk_p, sbv_p, pool_p, gla_p,
            mkp.reshape(mem_shape), mvp.reshape(mem_shape), sbk_s, sbv_s, pool_s, gla_s, gv_s)
```

```python
import functools
import math

import jax
import jax.numpy as jnp
from jax import lax
from jax.experimental import pallas as pl
from jax.experimental.pallas import tpu as pltpu

F32 = jnp.float32
BF16 = jnp.bfloat16
EPS = 1e-6

LANES = 128
PAGE_SIZE = 128
POOL_WINDOWS = (2, 4, 8, 16)
POOL_HALO = 16
POOL_GROUP_DIM = 128
SB_HEAD_DIM = 128
SB_BLOCK = 256
SB_QBLOCK = 512
SB_SLOTS = 16
GLA_HEADS = 4
GLA_DK = 128
GLA_DV = 256
GLA_RANK = 16
GLA_TAU = 16.0
GLA_CHUNK = 32
GMLP_GROUPS = 4
GMLP_GROUP_DIM = 256
GMLP_CHUNK = 128
MEM_HEADS = 4
MEM_HEAD_DIM = 128
SAMPLE_ROWS = 16
VMEM_LIMIT = 56 * 1024 * 1024


def _params(n_axes):
    return pltpu.CompilerParams(
        dimension_semantics=("arbitrary",) * n_axes, vmem_limit_bytes=VMEM_LIMIT)


def _tile(n, candidates):
    for c in candidates:
        if n % c == 0:
            return c
    return n


def _softplus(z):
    return jnp.maximum(z, 0.0) + jnp.log(1.0 + jnp.exp(-jnp.abs(z)))


def _split_bf16(x):
    hi = x.astype(BF16)
    lo = (x - hi.astype(F32)).astype(BF16)
    return hi, lo


def _nmm_kernel(x_ref, g_ref, w_ref, o_ref, xn_ref, *, act, rows):
    @pl.when(pl.program_id(1) == 0)
    def _():
        g = g_ref[...]

        def body(c, carry):
            r0 = pl.multiple_of(c * rows, rows)
            x = x_ref[pl.ds(r0, rows), :]
            ms = jnp.mean(x * x, axis=-1, keepdims=True)
            xn_ref[pl.ds(r0, rows), :] = (x * lax.rsqrt(ms + EPS) * g).astype(BF16)
            return carry

        lax.fori_loop(0, x_ref.shape[0] // rows, body, 0)

    y = jnp.dot(xn_ref[...], w_ref[...], preferred_element_type=F32)
    if act == "relu2":
        y = jnp.square(jnp.maximum(y, 0.0))
    o_ref[...] = y.astype(o_ref.dtype)


def _nmm(x, g, w, *, act=None, out_dtype=F32):
    M, K = x.shape
    N = w.shape[1]
    tm = _tile(M, (1024, 512, 256))
    tn = _tile(N, (1024, 768, 512, 256, 128))
    rows = _tile(tm, (128, 64, 32, 16, 8))
    return pl.pallas_call(
        functools.partial(_nmm_kernel, act=act, rows=rows),
        grid=(M // tm, N // tn),
        in_specs=[
            pl.BlockSpec((tm, K), lambda i, j: (i, 0)),
            pl.BlockSpec((1, K), lambda i, j: (0, 0)),
            pl.BlockSpec((K, tn), lambda i, j: (0, j)),
        ],
        out_specs=pl.BlockSpec((tm, tn), lambda i, j: (i, j)),
        out_shape=jax.ShapeDtypeStruct((M, N), out_dtype),
        scratch_shapes=[pltpu.VMEM((tm, K), BF16)],
        compiler_params=_params(2),
        name="norm_matmul",
    )(x, g.reshape(1, K), w)


def _mmres_kernel(*refs, n1, nk, two):
    if two:
        a1_ref, a2_ref, w_ref, res_ref, o_ref = refs
    else:
        a1_ref, w_ref, res_ref, o_ref = refs
        a2_ref = None
    k = pl.program_id(2)

    @pl.when(k == 0)
    def _():
        o_ref[...] = res_ref[...]

    @pl.when(k < n1)
    def _():
        o_ref[...] += jnp.dot(a1_ref[...], w_ref[...], preferred_element_type=F32)

    if two:
        @pl.when(k >= n1)
        def _():
            o_ref[...] += jnp.dot(a2_ref[...], w_ref[...], preferred_element_type=F32)


def _mm_res(a1, a2, w, res):
    M, K1 = a1.shape
    K2 = 0 if a2 is None else a2.shape[1]
    N = w.shape[1]
    tm = _tile(M, (1024, 512, 256))
    tn = _tile(N, (1024, 512, 256))
    tk = _tile(math.gcd(K1, K2) if K2 else K1, (1024, 512, 256, 128))
    n1 = K1 // tk
    nk = (K1 + K2) // tk
    two = a2 is not None
    in_specs = [pl.BlockSpec((tm, tk), lambda i, j, k: (i, jnp.minimum(k, n1 - 1)))]
    args = [a1]
    if two:
        in_specs.append(pl.BlockSpec((tm, tk), lambda i, j, k: (i, jnp.maximum(k - n1, 0))))
        args.append(a2)
    in_specs += [
        pl.BlockSpec((tk, tn), lambda i, j, k: (k, j)),
        pl.BlockSpec((tm, tn), lambda i, j, k: (i, j)),
    ]
    args += [w, res]
    return pl.pallas_call(
        functools.partial(_mmres_kernel, n1=n1, nk=nk, two=two),
        grid=(M // tm, N // tn, nk),
        in_specs=in_specs,
        out_specs=pl.BlockSpec((tm, tn), lambda i, j, k: (i, j)),
        out_shape=jax.ShapeDtypeStruct((M, N), F32),
        compiler_params=_params(3),
        name="matmul_residual",
    )(*args)


def _rms_kernel(x_ref, g_ref, o_ref):
    x = x_ref[...]
    ms = jnp.mean(x * x, axis=-1, keepdims=True)
    o_ref[...] = x * lax.rsqrt(ms + EPS) * g_ref[...]


def _rmsnorm(x, g):
    M, K = x.shape
    tm = _tile(M, (256, 128, 64, 8))
    return pl.pallas_call(
        _rms_kernel,
        grid=(M // tm,),
        in_specs=[pl.BlockSpec((tm, K), lambda i: (i, 0)), pl.BlockSpec((1, K), lambda i: (0, 0))],
        out_specs=pl.BlockSpec((tm, K), lambda i: (i, 0)),
        out_shape=jax.ShapeDtypeStruct((M, K), F32),
        compiler_params=_params(1),
        name="final_rmsnorm",
    )(x, g.reshape(1, K))


def _pool_kernel(ext_ref, w_ref, scale_ref, o_ref, diff_ref, *, bb, lc, pos_base, pos_stride):
    blk = pl.program_id(0)
    g = pl.program_id(1)
    win = jnp.where(g == 0, POOL_WINDOWS[0],
                    jnp.where(g == 1, POOL_WINDOWS[1],
                              jnp.where(g == 2, POOL_WINDOWS[2], POOL_WINDOWS[3])))
    t_iota = lax.broadcasted_iota(jnp.int32, (lc, LANES), 0)

    def body(b, carry):
        def e(i):
            return ext_ref[b, pl.ds(POOL_HALO - i, lc), :]

        cur = e(0)
        s2 = cur + e(1)
        s4 = s2 + e(2) + e(3)
        s8 = s4 + e(4) + e(5) + e(6) + e(7)
        s16 = s8 + e(8) + e(9) + e(10) + e(11) + e(12) + e(13) + e(14) + e(15)
        ws = jnp.where(g == 0, s2, jnp.where(g == 1, s4, jnp.where(g == 2, s8, s16)))
        pos = pos_base + (blk * bb + b) * pos_stride + t_iota
        cnt = jnp.minimum(win, pos + 1).astype(F32)
        r0 = pl.multiple_of(b * lc, lc)
        diff_ref[pl.ds(r0, lc), :] = ws / cnt - cur
        return carry

    lax.fori_loop(0, bb, body, 0)
    out = jnp.dot(diff_ref[...].astype(BF16), w_ref[0].astype(BF16), preferred_element_type=F32)
    o_ref[...] = (out * scale_ref[...]).astype(o_ref.dtype)


def _pool_mix(ext, w, scale, *, pos_base, pos_stride):
    NB, rows, width = ext.shape
    lc = rows - POOL_HALO
    bb = NB if NB * lc <= 2048 else _tile(NB, (4, 2, 1))
    G = width // POOL_GROUP_DIM
    return pl.pallas_call(
        functools.partial(_pool_kernel, bb=bb, lc=lc, pos_base=pos_base, pos_stride=pos_stride),
        grid=(NB // bb, G),
        in_specs=[
            pl.BlockSpec((bb, rows, POOL_GROUP_DIM), lambda i, g: (i, 0, g)),
            pl.BlockSpec((1, POOL_GROUP_DIM, POOL_GROUP_DIM), lambda i, g: (g, 0, 0)),
            pl.BlockSpec((1, POOL_GROUP_DIM), lambda i, g: (0, g)),
        ],
        out_specs=pl.BlockSpec((bb * lc, POOL_GROUP_DIM), lambda i, g: (i, g)),
        out_shape=jax.ShapeDtypeStruct((NB * lc, width), BF16),
        scratch_shapes=[pltpu.VMEM((bb * lc, POOL_GROUP_DIM), F32)],
        compiler_params=_params(2),
        name="pool_mix",
    )(ext, w, scale.reshape(1, width))


def _suffix_matrix(n):
    r = jnp.arange(n)[:, None]
    c = jnp.arange(n)[None, :]
    u = (r > c).astype(BF16)
    uj = jnp.concatenate([u, jnp.ones((n, n), BF16)], axis=1)
    return jnp.concatenate([uj, uj], axis=0)


def _sb_prompt_kernel(bias_ref, q_ref, k_ref, v_ref, uj_ref, o_ref, acc_ref, run_ref, *, scale):
    h = pl.program_id(0)
    i = pl.program_id(1)
    tk = SB_BLOCK
    half = tk // 2
    n_sub = SB_QBLOCK // tk
    bias = bias_ref[h]
    uj = uj_ref[...]
    row = lax.broadcasted_iota(jnp.int32, (tk, tk), 0)
    col = lax.broadcasted_iota(jnp.int32, (tk, tk), 1)
    acc_ref[...] = jnp.zeros_like(acc_ref)
    run_ref[...] = jnp.zeros_like(run_ref)

    def suffix(lk_half):
        hi, lo = _split_bf16(lk_half)
        r = jnp.dot(jnp.concatenate([hi, lo], axis=1), uj, preferred_element_type=F32)
        return r[:, :half], r[:, half:]

    def step(j, masked):
        start = pl.multiple_of(j * tk, tk)
        kb = k_ref[pl.ds(start, tk), :].astype(BF16)
        vb = v_ref[pl.ds(start, tk), :].astype(BF16)
        for s in range(n_sub):
            rows = slice(s * tk, (s + 1) * tk)
            q = q_ref[rows, :].astype(BF16)
            z = lax.dot_general(q, kb, (((1,), (1,)), ((), ())), preferred_element_type=F32)
            z = z * scale + bias
            sp = _softplus(z)
            if masked:
                mask = (start + col) < (i * SB_QBLOCK + s * tk + row)
                lk = jnp.where(mask, -sp, 0.0)
            else:
                lk = -sp
            la0, t0 = suffix(lk[:, :half])
            la1, t1 = suffix(lk[:, half:])
            run = run_ref[rows, :]
            la = jnp.concatenate([la0 + (t1 + run), la1 + run], axis=1)
            wgt = jnp.exp(z - sp + la)
            if masked:
                wgt = jnp.where(mask, wgt, 0.0)
            acc_ref[rows, :] += jnp.dot(wgt.astype(BF16), vb, preferred_element_type=F32)
            run_ref[rows, :] = run + t0 + t1

    for d in range(n_sub):
        step(i * n_sub + (n_sub - 1 - d), True)

    def body(jj, carry):
        step(i * n_sub - 1 - jj, False)
        return carry

    lax.fori_loop(0, i * n_sub, body, 0)
    o_ref[...] = acc_ref[...].astype(o_ref.dtype)


def _sb_prompt(p, bias, *, q_col, k_col, v_col, heads):
    L = p.shape[0]
    tq = SB_QBLOCK
    tk = SB_BLOCK
    scale = SB_HEAD_DIM ** -0.5
    return pl.pallas_call(
        functools.partial(_sb_prompt_kernel, scale=scale),
        grid=(heads, L // tq),
        in_specs=[
            pl.BlockSpec(memory_space=pltpu.SMEM),
            pl.BlockSpec((tq, SB_HEAD_DIM), lambda h, i: (i, q_col + h)),
            pl.BlockSpec((L, SB_HEAD_DIM), lambda h, i: (0, k_col + h)),
            pl.BlockSpec((L, SB_HEAD_DIM), lambda h, i: (0, v_col + h)),
            pl.BlockSpec((tk, tk), lambda h, i: (0, 0)),
        ],
        out_specs=pl.BlockSpec((tq, SB_HEAD_DIM), lambda h, i: (i, h)),
        out_shape=jax.ShapeDtypeStruct((L, heads * SB_HEAD_DIM), BF16),
        scratch_shapes=[pltpu.VMEM((tq, SB_HEAD_DIM), F32), pltpu.VMEM((tq, SB_BLOCK // 2), F32)],
        compiler_params=_params(2),
        name="sb_prompt",
    )(bias, p, p, p, _suffix_matrix(SB_BLOCK // 2))


def _sb_sample_kernel(pt_ref, bias_ref, q_ref, knew_ref, vnew_ref, *rest, pps, n_new, heads, scale):
    k_refs = rest[:pps]
    v_refs = rest[pps:2 * pps]
    uj_ref, o_ref, acc_ref, run_ref, kpad_ref, vpad_ref = rest[2 * pps:]
    s = pl.program_id(1)
    uj = uj_ref[...]
    qs = [q_ref[0, h].astype(BF16) for h in range(heads)]

    def process(kref, vref, mask):
        zs = []
        for h in range(heads):
            kh = kref[:, h, :].astype(BF16)
            zh = lax.dot_general(qs[h], kh, (((1,), (1,)), ((), ())), preferred_element_type=F32)
            zs.append(zh * scale + bias_ref[h])
        z = jnp.concatenate(zs, axis=0)
        sp = _softplus(z)
        lk = -sp if mask is None else jnp.where(mask, -sp, 0.0)
        hi, lo = _split_bf16(lk)
        r = jnp.dot(jnp.concatenate([hi, lo], axis=1), uj, preferred_element_type=F32)
        run = run_ref[...]
        w = jnp.exp(z - sp + r[:, :PAGE_SIZE] + run)
        if mask is not None:
            w = jnp.where(mask, w, 0.0)
        for h in range(heads):
            wh = w[h * SB_SLOTS:(h + 1) * SB_SLOTS].astype(BF16)
            acc_ref[h] += jnp.dot(wh, vref[:, h, :].astype(BF16), preferred_element_type=F32)
        run_ref[...] = run + r[:, PAGE_SIZE:]

    @pl.when(s == 0)
    def _():
        acc_ref[...] = jnp.zeros_like(acc_ref)
        run_ref[...] = jnp.zeros_like(run_ref)
        kpad_ref[...] = jnp.zeros_like(kpad_ref)
        vpad_ref[...] = jnp.zeros_like(vpad_ref)
        kpad_ref[0:8] = knew_ref[0]
        vpad_ref[0:8] = vnew_ref[0]
        shape = (heads * SB_SLOTS, PAGE_SIZE)
        t = lax.broadcasted_iota(jnp.int32, shape, 0) % SB_SLOTS
        key = lax.broadcasted_iota(jnp.int32, shape, 1)
        process(kpad_ref, vpad_ref, jnp.logical_and(key < t, key < n_new))

    for r in range(pps):
        process(k_refs[r], v_refs[r], None)

    @pl.when(s == pl.num_programs(1) - 1)
    def _():
        o_ref[0] = acc_ref[...]


def _sb_sample(q, knew, vnew, cache_k, cache_v, e, page_table, bias):
    B, L, H, Dh = q.shape
    NP = page_table.shape[1]
    pps = math.gcd(NP, 4)
    q16 = jnp.pad(jnp.transpose(q, (0, 2, 1, 3)), ((0, 0), (0, 0), (0, SB_SLOTS - L), (0, 0)))
    kn = jnp.pad(knew, ((0, 0), (0, 8 - L), (0, 0), (0, 0)))
    vn = jnp.pad(vnew, ((0, 0), (0, 8 - L), (0, 0), (0, 0)))

    def page_spec(rr):
        return pl.BlockSpec((None, None, PAGE_SIZE, H, Dh),
                            lambda b, s, pt: (e, pt[b, NP - 1 - (s * pps + rr)], 0, 0, 0))

    seq4 = lambda b, s, pt: (b, 0, 0, 0)
    grid_spec = pltpu.PrefetchScalarGridSpec(
        num_scalar_prefetch=1,
        grid=(B, NP // pps),
        in_specs=[
            pl.BlockSpec(memory_space=pltpu.SMEM),
            pl.BlockSpec((1, H, SB_SLOTS, Dh), seq4),
            pl.BlockSpec((1, 8, H, Dh), seq4),
            pl.BlockSpec((1, 8, H, Dh), seq4),
            *[page_spec(rr) for rr in range(pps)],
            *[page_spec(rr) for rr in range(pps)],
            pl.BlockSpec((2 * PAGE_SIZE, 2 * PAGE_SIZE), lambda b, s, pt: (0, 0)),
        ],
        out_specs=pl.BlockSpec((1, H, SB_SLOTS, Dh), seq4),
        scratch_shapes=[
            pltpu.VMEM((H, SB_SLOTS, Dh), F32),
            pltpu.VMEM((H * SB_SLOTS, PAGE_SIZE), F32),
            pltpu.VMEM((PAGE_SIZE, H, Dh), F32),
            pltpu.VMEM((PAGE_SIZE, H, Dh), F32),
        ],
    )
    out = pl.pallas_call(
        functools.partial(_sb_sample_kernel, pps=pps, n_new=L, heads=H, scale=Dh ** -0.5),
        grid_spec=grid_spec,
        out_shape=jax.ShapeDtypeStruct((B, H, SB_SLOTS, Dh), F32),
        compiler_params=_params(2),
        name="sb_sample",
    )(page_table, bias.astype(F32), q16, kn, vn, *([cache_k] * pps), *([cache_v] * pps),
      _suffix_matrix(PAGE_SIZE))
    return jnp.transpose(out[:, :, :L], (0, 2, 1, 3)).reshape(B * L, H * Dh).astype(BF16)


def _mem_kernel(q_ref, k_ref, v_ref, o_ref, *, scale):
    for h in range(MEM_HEADS):
        sl = slice(h * MEM_HEAD_DIM, (h + 1) * MEM_HEAD_DIM)
        q = q_ref[0, :, sl]
        k = k_ref[0, :, h, :].astype(BF16)
        v = v_ref[0, :, h, :].astype(BF16)
        s = lax.dot_general(q, k, (((1,), (1,)), ((), ())), preferred_element_type=F32) * scale
        m = jnp.max(s, axis=-1, keepdims=True)
        e = jnp.exp(s - m)
        p = e / jnp.sum(e, axis=-1, keepdims=True)
        o_ref[0, :, sl] = jnp.dot(p.astype(BF16), v, preferred_element_type=F32).astype(o_ref.dtype)


def _mem_attend(q, mk, mv):
    B, L, W = q.shape
    n_mem = mk.shape[1]
    tq = _tile(L, (512, 256, 128, 64, 32, 16))
    return pl.pallas_call(
        functools.partial(_mem_kernel, scale=MEM_HEAD_DIM ** -0.5),
        grid=(B, L // tq),
        in_specs=[
            pl.BlockSpec((1, tq, W), lambda b, i: (b, i, 0)),
            pl.BlockSpec((1, n_mem, MEM_HEADS, MEM_HEAD_DIM), lambda b, i: (b, 0, 0, 0)),
            pl.BlockSpec((1, n_mem, MEM_HEADS, MEM_HEAD_DIM), lambda b, i: (b, 0, 0, 0)),
        ],
        out_specs=pl.BlockSpec((1, tq, W), lambda b, i: (b, i, 0)),
        out_shape=jax.ShapeDtypeStruct((B, L, W), BF16),
        compiler_params=_params(2),
        name="mem_attend",
    )(q, mk, mv)


def _gla_kernel(q_ref, k_ref, v_ref, r_ref, gl_ref, wg_ref, bg_ref, gn_ref, lt_ref, s0_ref,
                o_ref, sout_ref, s_ref, *, chunk, n_valid, n_chunks):
    i = pl.program_id(1)
    tm = chunk * n_chunks

    @pl.when(i == 0)
    def _():
        s_ref[...] = s0_ref[0]

    x = jnp.dot(gl_ref[0].astype(BF16), wg_ref[...].astype(BF16),
                preferred_element_type=F32) + bg_ref[...]
    la = -_softplus(-x) / GLA_TAU
    if n_valid < chunk:
        row = lax.broadcasted_iota(jnp.int32, la.shape, 0) % chunk
        la = jnp.where(row < n_valid, la, 0.0)
    hi, lo = _split_bf16(la)
    bcum = jnp.dot(lt_ref[...], jnp.concatenate([hi, lo], axis=0), preferred_element_type=F32)
    tri = (lax.broadcasted_iota(jnp.int32, (chunk, chunk), 0)
           >= lax.broadcasted_iota(jnp.int32, (chunk, chunk), 1))
    gn = gn_ref[...]
    qscale = GLA_DK ** -0.5
    for c in range(n_chunks):
        rows = slice(c * chunk, (c + 1) * chunk)
        for h in range(GLA_HEADS):
            kcols = slice(h * GLA_DK, (h + 1) * GLA_DK)
            vcols = slice(h * GLA_DV, (h + 1) * GLA_DV)
            b = bcum[rows, kcols]
            bl = b[chunk - 1:chunk, :]
            qh = q_ref[0, rows, kcols] * qscale
            kh = k_ref[0, rows, kcols]
            vh = v_ref[0, rows, vcols].astype(BF16)
            qd = (qh * jnp.exp(b)).astype(BF16)
            kd = (kh * jnp.exp(-b)).astype(BF16)
            sc = lax.dot_general(qd, kd, (((1,), (1,)), ((), ())), preferred_element_type=F32)
            sc = jnp.where(tri, sc, 0.0).astype(BF16)
            st = s_ref[h]
            o = (jnp.dot(sc, vh, preferred_element_type=F32)
                 + jnp.dot(qd, st.astype(BF16), preferred_element_type=F32))
            ku = (kh * jnp.exp(bl - b)).astype(BF16)
            upd = lax.dot_general(ku, vh, (((0,), (0,)), ((), ())), preferred_element_type=F32)
            decay = jnp.exp(jnp.broadcast_to(bl, (GLA_DK, GLA_DK)).T)
            s_ref[h] = jnp.concatenate([decay, decay], axis=1) * st + upd
            ms = jnp.mean(o * o, axis=-1, keepdims=True)
            rh = r_ref[0, rows, vcols]
            gate = rh / (1.0 + jnp.exp(-rh))
            o_ref[0, rows, vcols] = (o * lax.rsqrt(ms + EPS) * gn * gate).astype(o_ref.dtype)

    @pl.when(i == pl.num_programs(1) - 1)
    def _():
        sout_ref[0] = s_ref[...]


def _gla(p, s0, w_gate, b_gate, gnorm, *, chunk, n_valid, gl_col):
    B, Lp, W = p.shape
    n_chunks = _tile(Lp // chunk, (4, 2, 1))
    tm = chunk * n_chunks
    kw = GLA_HEADS * GLA_DK
    vw = GLA_HEADS * GLA_DV
    glw = 256
    wg = jnp.pad(w_gate, ((0, glw - w_gate.shape[0]), (0, 0)))
    r = jnp.arange(tm)
    lt = ((r[:, None] >= r[None, :]) & (r[:, None] // chunk == r[None, :] // chunk)).astype(BF16)
    lt2 = jnp.concatenate([lt, lt], axis=1)
    o, s_out = pl.pallas_call(
        functools.partial(_gla_kernel, chunk=chunk, n_valid=n_valid, n_chunks=n_chunks),
        grid=(B, Lp // tm),
        in_specs=[
            pl.BlockSpec((1, tm, kw), lambda b, i: (b, i, 0)),
            pl.BlockSpec((1, tm, kw), lambda b, i: (b, i, 1)),
            pl.BlockSpec((1, tm, vw), lambda b, i: (b, i, 1)),
            pl.BlockSpec((1, tm, vw), lambda b, i: (b, i, 2)),
            pl.BlockSpec((1, tm, glw), lambda b, i: (b, i, gl_col // glw)),
            pl.BlockSpec((glw, kw), lambda b, i: (0, 0)),
            pl.BlockSpec((1, kw), lambda b, i: (0, 0)),
            pl.BlockSpec((1, GLA_DV), lambda b, i: (0, 0)),
            pl.BlockSpec((tm, 2 * tm), lambda b, i: (0, 0)),
            pl.BlockSpec((1, GLA_HEADS, GLA_DK, GLA_DV), lambda b, i: (b, 0, 0, 0)),
        ],
        out_specs=[
            pl.BlockSpec((1, tm, vw), lambda b, i: (b, i, 0)),
            pl.BlockSpec((1, GLA_HEADS, GLA_DK, GLA_DV), lambda b, i: (b, 0, 0, 0)),
        ],
        out_shape=[
            jax.ShapeDtypeStruct((B, Lp, vw), BF16),
            jax.ShapeDtypeStruct((B, GLA_HEADS, GLA_DK, GLA_DV), F32),
        ],
        scratch_shapes=[pltpu.VMEM((GLA_HEADS, GLA_DK, GLA_DV), F32)],
        compiler_params=_params(2),
        name="gla",
    )(p, p, p, p, p, wg, b_gate.reshape(1, kw), gnorm.reshape(1, GLA_DV), lt2, s0)
    return o, s_out


def _gmlp_kernel(u_ref, vg_ref, gn_ref, w_ref, b_ref, o_ref, vn_ref):
    vg = vg_ref[...]
    ms = jnp.mean(vg * vg, axis=-1, keepdims=True)
    vn = vg * lax.rsqrt(ms + EPS) * gn_ref[...]
    vn_ref[...] = vn
    n = w_ref.shape[1]
    tri = (lax.broadcasted_iota(jnp.int32, (n, n), 0) >= lax.broadcasted_iota(jnp.int32, (n, n), 1))
    for g in range(GMLP_GROUPS):
        cols = slice(g * GMLP_GROUP_DIM, (g + 1) * GMLP_GROUP_DIM)
        w = jnp.where(tri, w_ref[g], 0.0).astype(BF16)
        mixed = jnp.dot(w, vn[:, cols].astype(BF16), preferred_element_type=F32) + b_ref[:, cols]
        o_ref[:, cols] = (u_ref[:, cols] * mixed).astype(o_ref.dtype)


def _gmlp(p, gnorm, w, bias_full, *, u_col, vg_col):
    M = p.shape[0]
    n = GMLP_CHUNK
    width = GMLP_GROUPS * GMLP_GROUP_DIM
    return pl.pallas_call(
        _gmlp_kernel,
        grid=(M // n,),
        in_specs=[
            pl.BlockSpec((n, width), lambda i: (i, u_col)),
            pl.BlockSpec((n, width), lambda i: (i, vg_col)),
            pl.BlockSpec((1, width), lambda i: (0, 0)),
            pl.BlockSpec((GMLP_GROUPS, n, n), lambda i: (0, 0, 0)),
            pl.BlockSpec((n, width), lambda i: (0, 0)),
        ],
        out_specs=[pl.BlockSpec((n, width), lambda i: (i, 0)), pl.BlockSpec((n, width), lambda i: (i, 0))],
        out_shape=[jax.ShapeDtypeStruct((M, width), BF16), jax.ShapeDtypeStruct((M, width), F32)],
        compiler_params=_params(1),
        name="gmlp",
    )(p, p, gnorm.reshape(1, width), w, bias_full)


def _pad_rows(a, rows):
    return jnp.pad(a, ((0, 0), (0, rows - a.shape[1]), (0, 0)))


def _trunk(x, B, L, pos0, mem_k, mem_v, pool_state, gla_state, page_table, cache_sb_k, cache_sb_v, wts):
    sampling = page_table is not None
    M, D = x.shape
    depth = wts["norm_mix"].shape[0]
    pools, glas, ks, vs, gvs = [], [], [], [], []
    for l in range(depth):
        if l % 2 == 0:
            e = l // 2
            pool_w = wts["pool_w"][e]
            pw = pool_w.shape[0] * pool_w.shape[1]
            sbw = (wts["w_in_even"][e].shape[1] - pw) // 3
            heads = sbw // SB_HEAD_DIM
            p = _nmm(x, wts["norm_mix"][l], wts["w_in_even"][e])
            a = p[:, :pw].reshape(B, L, pw)
            k = p[:, pw + sbw:pw + 2 * sbw].reshape(B, L, heads, SB_HEAD_DIM)
            v = p[:, pw + 2 * sbw:].reshape(B, L, heads, SB_HEAD_DIM)
            if sampling:
                buf = pool_state[e]
                ext = jnp.concatenate(
                    [jnp.zeros((B, POOL_HALO - buf.shape[1], pw), F32), buf, _pad_rows(a, 8)], axis=1)
                pool_out = _pool_mix(ext, pool_w, wts["pool_scale"][e], pos_base=pos0, pos_stride=0)
                pool_out = pool_out.reshape(B, 8, pw)[:, :L].reshape(M, pw)
                new_buf = jnp.concatenate([buf, a], axis=1)[:, -buf.shape[1]:]
                q = p[:, pw:pw + sbw].reshape(B, L, heads, SB_HEAD_DIM)
                o = _sb_sample(q, k, v, cache_sb_k, cache_sb_v, e, page_table, wts["sb_bias"][e])
            else:
                lc = _tile(L, (256, 128, 64, 32, 16, 8))
                nb = L // lc
                full = jnp.concatenate([jnp.zeros((POOL_HALO, pw), F32), a[0]], axis=0)
                ext = jnp.stack([full[c * lc:c * lc + lc + POOL_HALO] for c in range(nb)])
                pool_out = _pool_mix(ext, pool_w, wts["pool_scale"][e], pos_base=0, pos_stride=lc)
                hist = POOL_HALO - 1
                new_buf = jnp.concatenate([jnp.zeros((B, hist, pw), F32), a], axis=1)[:, -hist:]
                o = _sb_prompt(p, wts["sb_bias"][e].astype(F32), q_col=pw // SB_HEAD_DIM,
                               k_col=(pw + sbw) // SB_HEAD_DIM, v_col=(pw + 2 * sbw) // SB_HEAD_DIM,
                               heads=heads)
            x = _mm_res(pool_out, o, wts["w_out_even"][e], x)
            pools.append(new_buf)
            ks.append(k)
            vs.append(v)
        else:
            o_ = l // 2
            p = _nmm(x, wts["norm_mix"][l], wts["w_in_odd"][o_])
            W = p.shape[1]
            gl_col = 2 * GLA_HEADS * GLA_DK + 2 * GLA_HEADS * GLA_DV + 2 * GMLP_GROUPS * GMLP_GROUP_DIM
            ws = wts["gmlp_ws"][o_]
            bs = wts["gmlp_bs"][o_]
            if sampling:
                rows = SAMPLE_ROWS
                pp = _pad_rows(p.reshape(B, L, W), rows)
                og, s_new = _gla(pp, gla_state[o_], wts["gla_w_gate"][o_], wts["gla_b_gate"][o_],
                                 wts["gla_norm"][o_], chunk=rows, n_valid=L, gl_col=gl_col)
                og = og[:, :L].reshape(M, -1)
                per = GMLP_CHUNK // rows
                w_small = jnp.pad(ws[:, :L, :L], ((0, 0), (0, rows - L), (0, rows - L)))
                sel = jnp.eye(per, dtype=bool)[None, :, None, :, None]
                w_bd = jnp.where(sel, w_small[:, None, :, None, :], 0.0).reshape(
                    GMLP_GROUPS, GMLP_CHUNK, GMLP_CHUNK)
                b_small = jnp.tile(jnp.pad(bs[:, :L], ((0, 0), (0, rows - L))), (1, per))
                bias_full = jnp.repeat(b_small.T, GMLP_GROUP_DIM, axis=1)
                gm, vn = _gmlp(pp.reshape(B * rows, W), wts["gmlp_norm"][o_], w_bd, bias_full,
                               u_col=3, vg_col=4)
                gm = gm.reshape(B, rows, -1)[:, :L].reshape(M, -1)
                vn = vn.reshape(B, rows, -1)[:, :L]
            else:
                chunk = math.gcd(L, GLA_CHUNK)
                s0 = jnp.zeros((B, GLA_HEADS, GLA_DK, GLA_DV), F32)
                og, s_new = _gla(p.reshape(B, L, W), s0, wts["gla_w_gate"][o_], wts["gla_b_gate"][o_],
                                 wts["gla_norm"][o_], chunk=chunk, n_valid=chunk, gl_col=gl_col)
                og = og.reshape(M, -1)
                bias_full = jnp.repeat(bs.T, GMLP_GROUP_DIM, axis=1)
                gm, vn = _gmlp(p, wts["gmlp_norm"][o_], ws, bias_full, u_col=3, vg_col=4)
                vn = vn.reshape(B, L, -1)
            x = _mm_res(og, gm, wts["w_out_odd"][o_], x)
            glas.append(s_new)
            gvs.append(vn)
        qm = _nmm(x, wts["norm_mem"][l], wts["mem_wq"][l], out_dtype=BF16)
        mw = qm.shape[1]
        if sampling:
            qm = _pad_rows(qm.reshape(B, L, mw), SAMPLE_ROWS)
            om = _mem_attend(qm, mem_k[l], mem_v[l])[:, :L].reshape(M, mw)
        else:
            om = _mem_attend(qm.reshape(B, L, mw), mem_k[l], mem_v[l]).reshape(M, mw)
        x = _mm_res(om, None, wts["mem_wo"][l], x)
        hdn = _nmm(x, wts["norm_ffn"][l], wts["ffn_w1"][l], act="relu2", out_dtype=BF16)
        x = _mm_res(hdn, None, wts["ffn_w2"][l], x)
    y = _rmsnorm(x, wts["norm_final"])
    return y, jnp.stack(ks), jnp.stack(vs), jnp.stack(pools), jnp.stack(glas), jnp.stack(gvs)


def _prep_w_in_odd(w):
    main = 2 * GLA_HEADS * GLA_DK + 2 * GLA_HEADS * GLA_DV
    gate = w[:, :, main:main + GLA_RANK]
    rest = w[:, :, main + GLA_RANK:]
    gate = jnp.pad(gate, ((0, 0), (0, 0), (0, 256 - GLA_RANK)))
    return jnp.concatenate([w[:, :, :main], rest, gate], axis=-1)


def kernel(x_prompt, x_sample, mem_prompt, cache_sb_k, cache_sb_v, page_table, state_pool, state_gla, cache_mem_k, cache_mem_v, norm_mix, norm_mem, norm_memkv, norm_ffn, norm_final, w_in_even, w_out_even, pool_w, pool_scale, sb_bias, w_in_odd, w_out_odd, gla_w_gate, gla_b_gate, gla_norm, gmlp_norm, gmlp_ws, gmlp_bs, mem_wq, mem_wk, mem_wv, mem_wo, ffn_w1, ffn_w2):
    bf = lambda w: w.astype(BF16)
    wts = {
        "norm_mix": norm_mix, "norm_mem": norm_mem, "norm_ffn": norm_ffn, "norm_final": norm_final,
        "w_in_even": bf(w_in_even), "w_out_even": bf(w_out_even), "pool_w": pool_w,
        "pool_scale": pool_scale, "sb_bias": sb_bias,
        "w_in_odd": bf(_prep_w_in_odd(w_in_odd)), "w_out_odd": bf(w_out_odd),
        "gla_w_gate": gla_w_gate, "gla_b_gate": gla_b_gate, "gla_norm": gla_norm,
        "gmlp_norm": gmlp_norm, "gmlp_ws": gmlp_ws, "gmlp_bs": gmlp_bs,
        "mem_wq": bf(mem_wq), "mem_wo": bf(mem_wo), "ffn_w1": bf(ffn_w1), "ffn_w2": bf(ffn_w2),
    }
    depth = norm_mix.shape[0]
    B, L, D = x_prompt.shape
    n_mem = mem_prompt.shape[1]
    mp = mem_prompt.reshape(B * n_mem, D)
    mkp = jnp.stack([_nmm(mp, norm_memkv[l], bf(mem_wk[l])) for l in range(depth)])
    mvp = jnp.stack([_nmm(mp, norm_memkv[l], bf(mem_wv[l])) for l in range(depth)])
    mem_shape = (depth, B, n_mem, MEM_HEADS, MEM_HEAD_DIM)
    mkp = mkp.reshape(mem_shape)
    mvp = mvp.reshape(mem_shape)
    y_p, sbk_p, sbv_p, pool_p, gla_p, _ = _trunk(
        x_prompt.reshape(B * L, D), B, L, 0, mkp, mvp, None, None, None, None, None, wts)
    Bs, Ls, _ = x_sample.shape
    past_len = page_table.shape[1] * PAGE_SIZE
    y_s, sbk_s, sbv_s, pool_s, gla_s, gv_s = _trunk(
        x_sample.reshape(Bs * Ls, D), Bs, Ls, past_len, cache_mem_k, cache_mem_v, state_pool, state_gla,
        page_table, cache_sb_k, cache_sb_v, wts)
    return (y_p.reshape(B, L, D), y_s.reshape(Bs, Ls, D), sbk_p, sbv_p, pool_p, gla_p,
            mkp, mvp, sbk_s, sbv_s, pool_s, gla_s, gv_s)
```

```python
import functools
import math

import jax
import jax.numpy as jnp
from jax import lax
from jax.experimental import pallas as pl
from jax.experimental.pallas import tpu as pltpu

F32 = jnp.float32
BF16 = jnp.bfloat16
EPS = 1e-6

LANES = 128
PAGE_SIZE = 128
POOL_WINDOWS = (2, 4, 8, 16)
POOL_HALO = 16
POOL_GROUP_DIM = 128
SB_HEAD_DIM = 128
SB_BLOCK = 256
SB_QBLOCK = 512
SB_SLOTS = 16
GLA_HEADS = 4
GLA_DK = 128
GLA_DV = 256
GLA_RANK = 16
GLA_TAU = 16.0
GLA_CHUNK = 32
GMLP_GROUPS = 4
GMLP_GROUP_DIM = 256
GMLP_CHUNK = 128
MEM_HEADS = 4
MEM_HEAD_DIM = 128
SAMPLE_ROWS = 16
VMEM_LIMIT = 56 * 1024 * 1024


def _params(n_axes):
    return pltpu.CompilerParams(
        dimension_semantics=("arbitrary",) * n_axes, vmem_limit_bytes=VMEM_LIMIT)


def _tile(n, candidates):
    for c in candidates:
        if n % c == 0:
            return c
    return n


def _softplus(z):
    return jnp.maximum(z, 0.0) + jnp.log(1.0 + jnp.exp(-jnp.abs(z)))


def _split_bf16(x):
    hi = x.astype(BF16)
    lo = (x - hi.astype(F32)).astype(BF16)
    return hi, lo


def _nmm_kernel(x_ref, g_ref, w_ref, o_ref, xn_ref, *, act, rows):
    @pl.when(pl.program_id(1) == 0)
    def _():
        g = g_ref[...]

        def body(c, carry):
            r0 = pl.multiple_of(c * rows, rows)
            x = x_ref[pl.ds(r0, rows), :]
            ms = jnp.mean(x * x, axis=-1, keepdims=True)
            xn_ref[pl.ds(r0, rows), :] = (x * lax.rsqrt(ms + EPS) * g).astype(BF16)
            return carry

        lax.fori_loop(0, x_ref.shape[0] // rows, body, 0)

    y = jnp.dot(xn_ref[...], w_ref[...], preferred_element_type=F32)
    if act == "relu2":
        y = jnp.square(jnp.maximum(y, 0.0))
    o_ref[...] = y.astype(o_ref.dtype)


def _nmm(x, g, w, *, act=None, out_dtype=F32):
    M, K = x.shape
    N = w.shape[1]
    tm = _tile(M, (1024, 512, 256))
    tn = _tile(N, (1024, 768, 512, 256, 128))
    rows = _tile(tm, (128, 64, 32, 16, 8))
    return pl.pallas_call(
        functools.partial(_nmm_kernel, act=act, rows=rows),
        grid=(M // tm, N // tn),
        in_specs=[
            pl.BlockSpec((tm, K), lambda i, j: (i, 0)),
            pl.BlockSpec((1, K), lambda i, j: (0, 0)),
            pl.BlockSpec((K, tn), lambda i, j: (0, j)),
        ],
        out_specs=pl.BlockSpec((tm, tn), lambda i, j: (i, j)),
        out_shape=jax.ShapeDtypeStruct((M, N), out_dtype),
        scratch_shapes=[pltpu.VMEM((tm, K), BF16)],
        compiler_params=_params(2),
        name="norm_matmul",
    )(x, g.reshape(1, K), w)


def _mmres_kernel(*refs, n1, nk, two):
    if two:
        a1_ref, a2_ref, w_ref, res_ref, o_ref = refs
    else:
        a1_ref, w_ref, res_ref, o_ref = refs
        a2_ref = None
    k = pl.program_id(2)

    @pl.when(k == 0)
    def _():
        o_ref[...] = res_ref[...]

    @pl.when(k < n1)
    def _():
        o_ref[...] += jnp.dot(a1_ref[...], w_ref[...], preferred_element_type=F32)

    if two:
        @pl.when(k >= n1)
        def _():
            o_ref[...] += jnp.dot(a2_ref[...], w_ref[...], preferred_element_type=F32)


def _mm_res(a1, a2, w, res):
    M, K1 = a1.shape
    K2 = 0 if a2 is None else a2.shape[1]
    N = w.shape[1]
    tm = _tile(M, (1024, 512, 256))
    tn = _tile(N, (1024, 512, 256))
    tk = _tile(math.gcd(K1, K2) if K2 else K1, (1024, 512, 256, 128))
    n1 = K1 // tk
    nk = (K1 + K2) // tk
    two = a2 is not None
    in_specs = [pl.BlockSpec((tm, tk), lambda i, j, k: (i, jnp.minimum(k, n1 - 1)))]
    args = [a1]
    if two:
        in_specs.append(pl.BlockSpec((tm, tk), lambda i, j, k: (i, jnp.maximum(k - n1, 0))))
        args.append(a2)
    in_specs += [
        pl.BlockSpec((tk, tn), lambda i, j, k: (k, j)),
        pl.BlockSpec((tm, tn), lambda i, j, k: (i, j)),
    ]
    args += [w, res]
    return pl.pallas_call(
        functools.partial(_mmres_kernel, n1=n1, nk=nk, two=two),
        grid=(M // tm, N // tn, nk),
        in_specs=in_specs,
        out_specs=pl.BlockSpec((tm, tn), lambda i, j, k: (i, j)),
        out_shape=jax.ShapeDtypeStruct((M, N), F32),
        compiler_params=_params(3),
        name="matmul_residual",
    )(*args)


def _rms_kernel(x_ref, g_ref, o_ref):
    x = x_ref[...]
    ms = jnp.mean(x * x, axis=-1, keepdims=True)
    o_ref[...] = x * lax.rsqrt(ms + EPS) * g_ref[...]


def _rmsnorm(x, g):
    M, K = x.shape
    tm = _tile(M, (256, 128, 64, 8))
    return pl.pallas_call(
        _rms_kernel,
        grid=(M // tm,),
        in_specs=[pl.BlockSpec((tm, K), lambda i: (i, 0)), pl.BlockSpec((1, K), lambda i: (0, 0))],
        out_specs=pl.BlockSpec((tm, K), lambda i: (i, 0)),
        out_shape=jax.ShapeDtypeStruct((M, K), F32),
        compiler_params=_params(1),
        name="final_rmsnorm",
    )(x, g.reshape(1, K))


def _pool_kernel(ext_ref, w_ref, scale_ref, o_ref, diff_ref, *, bb, lc, pos_base, pos_stride):
    blk = pl.program_id(0)
    g = pl.program_id(1)
    win = jnp.where(g == 0, POOL_WINDOWS[0],
                    jnp.where(g == 1, POOL_WINDOWS[1],
                              jnp.where(g == 2, POOL_WINDOWS[2], POOL_WINDOWS[3])))
    t_iota = lax.broadcasted_iota(jnp.int32, (lc, LANES), 0)

    def body(b, carry):
        def e(i):
            return ext_ref[b, pl.ds(POOL_HALO - i, lc), :]

        cur = e(0)
        s2 = cur + e(1)
        s4 = s2 + e(2) + e(3)
        s8 = s4 + e(4) + e(5) + e(6) + e(7)
        s16 = s8 + e(8) + e(9) + e(10) + e(11) + e(12) + e(13) + e(14) + e(15)
        ws = jnp.where(g == 0, s2, jnp.where(g == 1, s4, jnp.where(g == 2, s8, s16)))
        pos = pos_base + (blk * bb + b) * pos_stride + t_iota
        cnt = jnp.minimum(win, pos + 1).astype(F32)
        r0 = pl.multiple_of(b * lc, lc)
        diff_ref[pl.ds(r0, lc), :] = ws / cnt - cur
        return carry

    lax.fori_loop(0, bb, body, 0)
    out = jnp.dot(diff_ref[...].astype(BF16), w_ref[0].astype(BF16), preferred_element_type=F32)
    o_ref[...] = (out * scale_ref[...]).astype(o_ref.dtype)


def _pool_mix(ext, w, scale, *, pos_base, pos_stride):
    NB, rows, width = ext.shape
    lc = rows - POOL_HALO
    bb = NB if NB * lc <= 2048 else _tile(NB, (4, 2, 1))
    G = width // POOL_GROUP_DIM
    return pl.pallas_call(
        functools.partial(_pool_kernel, bb=bb, lc=lc, pos_base=pos_base, pos_stride=pos_stride),
        grid=(NB // bb, G),
        in_specs=[
            pl.BlockSpec((bb, rows, POOL_GROUP_DIM), lambda i, g: (i, 0, g)),
            pl.BlockSpec((1, POOL_GROUP_DIM, POOL_GROUP_DIM), lambda i, g: (g, 0, 0)),
            pl.BlockSpec((1, POOL_GROUP_DIM), lambda i, g: (0, g)),
        ],
        out_specs=pl.BlockSpec((bb * lc, POOL_GROUP_DIM), lambda i, g: (i, g)),
        out_shape=jax.ShapeDtypeStruct((NB * lc, width), BF16),
        scratch_shapes=[pltpu.VMEM((bb * lc, POOL_GROUP_DIM), F32)],
        compiler_params=_params(2),
        name="pool_mix",
    )(ext, w, scale.reshape(1, width))


def _suffix_matrix(n):
    r = jnp.arange(n)[:, None]
    c = jnp.arange(n)[None, :]
    u = (r > c).astype(BF16)
    uj = jnp.concatenate([u, jnp.ones((n, n), BF16)], axis=1)
    return jnp.concatenate([uj, uj], axis=0)


def _sb_prompt_kernel(bias_ref, q_ref, k_ref, v_ref, uj_ref, o_ref, acc_ref, run_ref, *, scale):
    h = pl.program_id(0)
    i = pl.program_id(1)
    tk = SB_BLOCK
    half = tk // 2
    n_sub = SB_QBLOCK // tk
    bias = bias_ref[h]
    uj = uj_ref[...]
    row = lax.broadcasted_iota(jnp.int32, (tk, tk), 0)
    col = lax.broadcasted_iota(jnp.int32, (tk, tk), 1)
    acc_ref[...] = jnp.zeros_like(acc_ref)
    run_ref[...] = jnp.zeros_like(run_ref)

    def suffix(lk_half):
        hi, lo = _split_bf16(lk_half)
        r = jnp.dot(jnp.concatenate([hi, lo], axis=1), uj, preferred_element_type=F32)
        return r[:, :half], r[:, half:]

    def step(j, masked):
        start = pl.multiple_of(j * tk, tk)
        kb = k_ref[pl.ds(start, tk), :].astype(BF16)
        vb = v_ref[pl.ds(start, tk), :].astype(BF16)
        for s in range(n_sub):
            rows = slice(s * tk, (s + 1) * tk)
            q = q_ref[rows, :].astype(BF16)
            z = lax.dot_general(q, kb, (((1,), (1,)), ((), ())), preferred_element_type=F32)
            z = z * scale + bias
            sp = _softplus(z)
            if masked:
                mask = (start + col) < (i * SB_QBLOCK + s * tk + row)
                lk = jnp.where(mask, -sp, 0.0)
            else:
                lk = -sp
            la0, t0 = suffix(lk[:, :half])
            la1, t1 = suffix(lk[:, half:])
            run = run_ref[rows, :]
            la = jnp.concatenate([la0 + (t1 + run), la1 + run], axis=1)
            wgt = jnp.exp(z - sp + la)
            if masked:
                wgt = jnp.where(mask, wgt, 0.0)
            acc_ref[rows, :] += jnp.dot(wgt.astype(BF16), vb, preferred_element_type=F32)
            run_ref[rows, :] = run + t0 + t1

    for d in range(n_sub):
        step(i * n_sub + (n_sub - 1 - d), True)

    def body(jj, carry):
        step(i * n_sub - 1 - jj, False)
        return carry

    lax.fori_loop(0, i * n_sub, body, 0)
    o_ref[...] = acc_ref[...].astype(o_ref.dtype)


def _sb_prompt(p, bias, *, q_col, k_col, v_col, heads):
    L = p.shape[0]
    tq = SB_QBLOCK
    tk = SB_BLOCK
    scale = SB_HEAD_DIM ** -0.5
    return pl.pallas_call(
        functools.partial(_sb_prompt_kernel, scale=scale),
        grid=(heads, L // tq),
        in_specs=[
            pl.BlockSpec(memory_space=pltpu.SMEM),
            pl.BlockSpec((tq, SB_HEAD_DIM), lambda h, i: (i, q_col + h)),
            pl.BlockSpec((L, SB_HEAD_DIM), lambda h, i: (0, k_col + h)),
            pl.BlockSpec((L, SB_HEAD_DIM), lambda h, i: (0, v_col + h)),
            pl.BlockSpec((tk, tk), lambda h, i: (0, 0)),
        ],
        out_specs=pl.BlockSpec((tq, SB_HEAD_DIM), lambda h, i: (i, h)),
        out_shape=jax.ShapeDtypeStruct((L, heads * SB_HEAD_DIM), BF16),
        scratch_shapes=[pltpu.VMEM((tq, SB_HEAD_DIM), F32), pltpu.VMEM((tq, SB_BLOCK // 2), F32)],
        compiler_params=_params(2),
        name="sb_prompt",
    )(bias, p, p, p, _suffix_matrix(SB_BLOCK // 2))


def _sb_sample_kernel(pt_ref, bias_ref, q_ref, knew_ref, vnew_ref, *rest, pps, n_new, heads, scale):
    k_refs = rest[:pps]
    v_refs = rest[pps:2 * pps]
    uj_ref, o_ref, acc_ref, run_ref, kpad_ref, vpad_ref = rest[2 * pps:]
    s = pl.program_id(1)
    uj = uj_ref[...]
    qs = [q_ref[0, h].astype(BF16) for h in range(heads)]

    def process(kref, vref, mask):
        zs = []
        for h in range(heads):
            kh = kref[h].astype(BF16)
            zh = lax.dot_general(qs[h], kh, (((1,), (1,)), ((), ())), preferred_element_type=F32)
            zs.append(zh * scale + bias_ref[h])
        z = jnp.concatenate(zs, axis=0)
        sp = _softplus(z)
        lk = -sp if mask is None else jnp.where(mask, -sp, 0.0)
        hi, lo = _split_bf16(lk)
        r = jnp.dot(jnp.concatenate([hi, lo], axis=1), uj, preferred_element_type=F32)
        run = run_ref[...]
        w = jnp.exp(z - sp + r[:, :PAGE_SIZE] + run)
        if mask is not None:
            w = jnp.where(mask, w, 0.0)
        for h in range(heads):
            wh = w[h * SB_SLOTS:(h + 1) * SB_SLOTS].astype(BF16)
            acc_ref[h] += jnp.dot(wh, vref[h].astype(BF16), preferred_element_type=F32)
        run_ref[...] = run + r[:, PAGE_SIZE:]

    @pl.when(s == 0)
    def _():
        acc_ref[...] = jnp.zeros_like(acc_ref)
        run_ref[...] = jnp.zeros_like(run_ref)
        kpad_ref[...] = jnp.zeros_like(kpad_ref)
        vpad_ref[...] = jnp.zeros_like(vpad_ref)
        kpad_ref[:, 0:8, :] = knew_ref[0]
        vpad_ref[:, 0:8, :] = vnew_ref[0]
        shape = (heads * SB_SLOTS, PAGE_SIZE)
        t = lax.broadcasted_iota(jnp.int32, shape, 0) % SB_SLOTS
        key = lax.broadcasted_iota(jnp.int32, shape, 1)
        process(kpad_ref, vpad_ref, jnp.logical_and(key < t, key < n_new))

    for r in range(pps):
        process(k_refs[r], v_refs[r], None)

    @pl.when(s == pl.num_programs(1) - 1)
    def _():
        o_ref[0] = acc_ref[...]


def _sb_sample(q, knew, vnew, cache_k, cache_v, e, page_table, bias):
    B, L, H, Dh = q.shape
    NP = page_table.shape[1]
    pps = math.gcd(NP, 4)
    q16 = jnp.pad(jnp.transpose(q, (0, 2, 1, 3)), ((0, 0), (0, 0), (0, SB_SLOTS - L), (0, 0)))
    kn = jnp.pad(jnp.transpose(knew, (0, 2, 1, 3)), ((0, 0), (0, 0), (0, 8 - L), (0, 0)))
    vn = jnp.pad(jnp.transpose(vnew, (0, 2, 1, 3)), ((0, 0), (0, 0), (0, 8 - L), (0, 0)))
    cache_k = jnp.transpose(cache_k, (0, 1, 3, 2, 4))
    cache_v = jnp.transpose(cache_v, (0, 1, 3, 2, 4))

    def page_spec(rr):
        return pl.BlockSpec((None, None, H, PAGE_SIZE, Dh),
                            lambda b, s, pt: (e, pt[b, NP - 1 - (s * pps + rr)], 0, 0, 0))

    seq4 = lambda b, s, pt: (b, 0, 0, 0)
    grid_spec = pltpu.PrefetchScalarGridSpec(
        num_scalar_prefetch=1,
        grid=(B, NP // pps),
        in_specs=[
            pl.BlockSpec(memory_space=pltpu.SMEM),
            pl.BlockSpec((1, H, SB_SLOTS, Dh), seq4),
            pl.BlockSpec((1, H, 8, Dh), seq4),
            pl.BlockSpec((1, H, 8, Dh), seq4),
            *[page_spec(rr) for rr in range(pps)],
            *[page_spec(rr) for rr in range(pps)],
            pl.BlockSpec((2 * PAGE_SIZE, 2 * PAGE_SIZE), lambda b, s, pt: (0, 0)),
        ],
        out_specs=pl.BlockSpec((1, H, SB_SLOTS, Dh), seq4),
        scratch_shapes=[
            pltpu.VMEM((H, SB_SLOTS, Dh), F32),
            pltpu.VMEM((H * SB_SLOTS, PAGE_SIZE), F32),
            pltpu.VMEM((H, PAGE_SIZE, Dh), F32),
            pltpu.VMEM((H, PAGE_SIZE, Dh), F32),
        ],
    )
    out = pl.pallas_call(
        functools.partial(_sb_sample_kernel, pps=pps, n_new=L, heads=H, scale=Dh ** -0.5),
        grid_spec=grid_spec,
        out_shape=jax.ShapeDtypeStruct((B, H, SB_SLOTS, Dh), F32),
        compiler_params=_params(2),
        name="sb_sample",
    )(page_table, bias.astype(F32), q16, kn, vn, *([cache_k] * pps), *([cache_v] * pps),
      _suffix_matrix(PAGE_SIZE))
    return jnp.transpose(out[:, :, :L], (0, 2, 1, 3)).reshape(B * L, H * Dh).astype(BF16)


def _mem_kernel(q_ref, k_ref, v_ref, o_ref, *, scale):
    for h in range(MEM_HEADS):
        sl = slice(h * MEM_HEAD_DIM, (h + 1) * MEM_HEAD_DIM)
        q = q_ref[0, :, sl]
        k = k_ref[0, :, h, :].astype(BF16)
        v = v_ref[0, :, h, :].astype(BF16)
        s = lax.dot_general(q, k, (((1,), (1,)), ((), ())), preferred_element_type=F32) * scale
        m = jnp.max(s, axis=-1, keepdims=True)
        e = jnp.exp(s - m)
        p = e / jnp.sum(e, axis=-1, keepdims=True)
        o_ref[0, :, sl] = jnp.dot(p.astype(BF16), v, preferred_element_type=F32).astype(o_ref.dtype)


def _mem_attend(q, mk, mv):
    B, L, W = q.shape
    n_mem = mk.shape[1]
    tq = _tile(L, (512, 256, 128, 64, 32, 16))
    return pl.pallas_call(
        functools.partial(_mem_kernel, scale=MEM_HEAD_DIM ** -0.5),
        grid=(B, L // tq),
        in_specs=[
            pl.BlockSpec((1, tq, W), lambda b, i: (b, i, 0)),
            pl.BlockSpec((1, n_mem, MEM_HEADS, MEM_HEAD_DIM), lambda b, i: (b, 0, 0, 0)),
            pl.BlockSpec((1, n_mem, MEM_HEADS, MEM_HEAD_DIM), lambda b, i: (b, 0, 0, 0)),
        ],
        out_specs=pl.BlockSpec((1, tq, W), lambda b, i: (b, i, 0)),
        out_shape=jax.ShapeDtypeStruct((B, L, W), BF16),
        compiler_params=_params(2),
        name="mem_attend",
    )(q, mk, mv)


def _gla_kernel(q_ref, k_ref, v_ref, r_ref, gl_ref, wg_ref, bg_ref, gn_ref, lt_ref, s0_ref,
                o_ref, sout_ref, s_ref, *, chunk, n_valid, n_chunks):
    i = pl.program_id(1)
    tm = chunk * n_chunks

    @pl.when(i == 0)
    def _():
        s_ref[...] = s0_ref[0]

    x = jnp.dot(gl_ref[0].astype(BF16), wg_ref[...].astype(BF16),
                preferred_element_type=F32) + bg_ref[...]
    la = -_softplus(-x) / GLA_TAU
    if n_valid < chunk:
        row = lax.broadcasted_iota(jnp.int32, la.shape, 0) % chunk
        la = jnp.where(row < n_valid, la, 0.0)
    hi, lo = _split_bf16(la)
    bcum = jnp.dot(lt_ref[...], jnp.concatenate([hi, lo], axis=0), preferred_element_type=F32)
    tri = (lax.broadcasted_iota(jnp.int32, (chunk, chunk), 0)
           >= lax.broadcasted_iota(jnp.int32, (chunk, chunk), 1))
    gn = gn_ref[...]
    qscale = GLA_DK ** -0.5
    for c in range(n_chunks):
        rows = slice(c * chunk, (c + 1) * chunk)
        for h in range(GLA_HEADS):
            kcols = slice(h * GLA_DK, (h + 1) * GLA_DK)
            vcols = slice(h * GLA_DV, (h + 1) * GLA_DV)
            b = bcum[rows, kcols]
            bl = b[chunk - 1:chunk, :]
            qh = q_ref[0, rows, kcols] * qscale
            kh = k_ref[0, rows, kcols]
            vh = v_ref[0, rows, vcols].astype(BF16)
            qd = (qh * jnp.exp(b)).astype(BF16)
            kd = (kh * jnp.exp(-b)).astype(BF16)
            sc = lax.dot_general(qd, kd, (((1,), (1,)), ((), ())), preferred_element_type=F32)
            sc = jnp.where(tri, sc, 0.0).astype(BF16)
            st = s_ref[h]
            o = (jnp.dot(sc, vh, preferred_element_type=F32)
                 + jnp.dot(qd, st.astype(BF16), preferred_element_type=F32))
            ku = (kh * jnp.exp(bl - b)).astype(BF16)
            upd = lax.dot_general(ku, vh, (((0,), (0,)), ((), ())), preferred_element_type=F32)
            decay = jnp.exp(jnp.broadcast_to(bl, (GLA_DK, GLA_DK)).T)
            s_ref[h] = jnp.concatenate([decay, decay], axis=1) * st + upd
            ms = jnp.mean(o * o, axis=-1, keepdims=True)
            rh = r_ref[0, rows, vcols]
            gate = rh / (1.0 + jnp.exp(-rh))
            o_ref[0, rows, vcols] = (o * lax.rsqrt(ms + EPS) * gn * gate).astype(o_ref.dtype)

    @pl.when(i == pl.num_programs(1) - 1)
    def _():
        sout_ref[0] = s_ref[...]


def _gla(p, s0, w_gate, b_gate, gnorm, *, chunk, n_valid, gl_col):
    B, Lp, W = p.shape
    n_chunks = _tile(Lp // chunk, (4, 2, 1))
    tm = chunk * n_chunks
    kw = GLA_HEADS * GLA_DK
    vw = GLA_HEADS * GLA_DV
    glw = 256
    wg = jnp.pad(w_gate, ((0, glw - w_gate.shape[0]), (0, 0)))
    r = jnp.arange(tm)
    lt = ((r[:, None] >= r[None, :]) & (r[:, None] // chunk == r[None, :] // chunk)).astype(BF16)
    lt2 = jnp.concatenate([lt, lt], axis=1)
    o, s_out = pl.pallas_call(
        functools.partial(_gla_kernel, chunk=chunk, n_valid=n_valid, n_chunks=n_chunks),
        grid=(B, Lp // tm),
        in_specs=[
            pl.BlockSpec((1, tm, kw), lambda b, i: (b, i, 0)),
            pl.BlockSpec((1, tm, kw), lambda b, i: (b, i, 1)),
            pl.BlockSpec((1, tm, vw), lambda b, i: (b, i, 1)),
            pl.BlockSpec((1, tm, vw), lambda b, i: (b, i, 2)),
            pl.BlockSpec((1, tm, glw), lambda b, i: (b, i, gl_col // glw)),
            pl.BlockSpec((glw, kw), lambda b, i: (0, 0)),
            pl.BlockSpec((1, kw), lambda b, i: (0, 0)),
            pl.BlockSpec((1, GLA_DV), lambda b, i: (0, 0)),
            pl.BlockSpec((tm, 2 * tm), lambda b, i: (0, 0)),
            pl.BlockSpec((1, GLA_HEADS, GLA_DK, GLA_DV), lambda b, i: (b, 0, 0, 0)),
        ],
        out_specs=[
            pl.BlockSpec((1, tm, vw), lambda b, i: (b, i, 0)),
            pl.BlockSpec((1, GLA_HEADS, GLA_DK, GLA_DV), lambda b, i: (b, 0, 0, 0)),
        ],
        out_shape=[
            jax.ShapeDtypeStruct((B, Lp, vw), BF16),
            jax.ShapeDtypeStruct((B, GLA_HEADS, GLA_DK, GLA_DV), F32),
        ],
        scratch_shapes=[pltpu.VMEM((GLA_HEADS, GLA_DK, GLA_DV), F32)],
        compiler_params=_params(2),
        name="gla",
    )(p, p, p, p, p, wg, b_gate.reshape(1, kw), gnorm.reshape(1, GLA_DV), lt2, s0)
    return o, s_out


def _gmlp_kernel(u_ref, vg_ref, gn_ref, w_ref, b_ref, o_ref, vn_ref):
    vg = vg_ref[...]
    ms = jnp.mean(vg * vg, axis=-1, keepdims=True)
    vn = vg * lax.rsqrt(ms + EPS) * gn_ref[...]
    vn_ref[...] = vn
    n = w_ref.shape[1]
    tri = (lax.broadcasted_iota(jnp.int32, (n, n), 0) >= lax.broadcasted_iota(jnp.int32, (n, n), 1))
    for g in range(GMLP_GROUPS):
        cols = slice(g * GMLP_GROUP_DIM, (g + 1) * GMLP_GROUP_DIM)
        w = jnp.where(tri, w_ref[g], 0.0).astype(BF16)
        mixed = jnp.dot(w, vn[:, cols].astype(BF16), preferred_element_type=F32) + b_ref[:, cols]
        o_ref[:, cols] = (u_ref[:, cols] * mixed).astype(o_ref.dtype)


def _gmlp(p, gnorm, w, bias_full, *, u_col, vg_col):
    M = p.shape[0]
    n = GMLP_CHUNK
    width = GMLP_GROUPS * GMLP_GROUP_DIM
    return pl.pallas_call(
        _gmlp_kernel,
        grid=(M // n,),
        in_specs=[
            pl.BlockSpec((n, width), lambda i: (i, u_col)),
            pl.BlockSpec((n, width), lambda i: (i, vg_col)),
            pl.BlockSpec((1, width), lambda i: (0, 0)),
            pl.BlockSpec((GMLP_GROUPS, n, n), lambda i: (0, 0, 0)),
            pl.BlockSpec((n, width), lambda i: (0, 0)),
        ],
        out_specs=[pl.BlockSpec((n, width), lambda i: (i, 0)), pl.BlockSpec((n, width), lambda i: (i, 0))],
        out_shape=[jax.ShapeDtypeStruct((M, width), BF16), jax.ShapeDtypeStruct((M, width), F32)],
        compiler_params=_params(1),
        name="gmlp",
    )(p, p, gnorm.reshape(1, width), w, bias_full)


def _pad_rows(a, rows):
    return jnp.pad(a, ((0, 0), (0, rows - a.shape[1]), (0, 0)))


def _trunk(x, B, L, pos0, mem_k, mem_v, pool_state, gla_state, page_table, cache_sb_k, cache_sb_v, wts):
    sampling = page_table is not None
    M, D = x.shape
    depth = wts["norm_mix"].shape[0]
    pools, glas, ks, vs, gvs = [], [], [], [], []
    for l in range(depth):
        if l % 2 == 0:
            e = l // 2
            pool_w = wts["pool_w"][e]
            pw = pool_w.shape[0] * pool_w.shape[1]
            sbw = (wts["w_in_even"][e].shape[1] - pw) // 3
            heads = sbw // SB_HEAD_DIM
            p = _nmm(x, wts["norm_mix"][l], wts["w_in_even"][e])
            a = p[:, :pw].reshape(B, L, pw)
            k = p[:, pw + sbw:pw + 2 * sbw].reshape(B, L, heads, SB_HEAD_DIM)
            v = p[:, pw + 2 * sbw:].reshape(B, L, heads, SB_HEAD_DIM)
            if sampling:
                buf = pool_state[e]
                ext = jnp.concatenate(
                    [jnp.zeros((B, POOL_HALO - buf.shape[1], pw), F32), buf, _pad_rows(a, 8)], axis=1)
                pool_out = _pool_mix(ext, pool_w, wts["pool_scale"][e], pos_base=pos0, pos_stride=0)
                pool_out = pool_out.reshape(B, 8, pw)[:, :L].reshape(M, pw)
                new_buf = jnp.concatenate([buf, a], axis=1)[:, -buf.shape[1]:]
                q = p[:, pw:pw + sbw].reshape(B, L, heads, SB_HEAD_DIM)
                o = _sb_sample(q, k, v, cache_sb_k, cache_sb_v, e, page_table, wts["sb_bias"][e])
            else:
                lc = _tile(L, (256, 128, 64, 32, 16, 8))
                nb = L // lc
                full = jnp.concatenate([jnp.zeros((POOL_HALO, pw), F32), a[0]], axis=0)
                ext = jnp.stack([full[c * lc:c * lc + lc + POOL_HALO] for c in range(nb)])
                pool_out = _pool_mix(ext, pool_w, wts["pool_scale"][e], pos_base=0, pos_stride=lc)
                hist = POOL_HALO - 1
                new_buf = jnp.concatenate([jnp.zeros((B, hist, pw), F32), a], axis=1)[:, -hist:]
                o = _sb_prompt(p, wts["sb_bias"][e].astype(F32), q_col=pw // SB_HEAD_DIM,
                               k_col=(pw + sbw) // SB_HEAD_DIM, v_col=(pw + 2 * sbw) // SB_HEAD_DIM,
                               heads=heads)
            x = _mm_res(pool_out, o, wts["w_out_even"][e], x)
            pools.append(new_buf)
            ks.append(k)
            vs.append(v)
        else:
            o_ = l // 2
            p = _nmm(x, wts["norm_mix"][l], wts["w_in_odd"][o_])
            W = p.shape[1]
            gl_col = 2 * GLA_HEADS * GLA_DK + 2 * GLA_HEADS * GLA_DV + 2 * GMLP_GROUPS * GMLP_GROUP_DIM
            ws = wts["gmlp_ws"][o_]
            bs = wts["gmlp_bs"][o_]
            if sampling:
                rows = SAMPLE_ROWS
                pp = _pad_rows(p.reshape(B, L, W), rows)
                og, s_new = _gla(pp, gla_state[o_], wts["gla_w_gate"][o_], wts["gla_b_gate"][o_],
                                 wts["gla_norm"][o_], chunk=rows, n_valid=L, gl_col=gl_col)
                og = og[:, :L].reshape(M, -1)
                per = GMLP_CHUNK // rows
                w_small = jnp.pad(ws[:, :L, :L], ((0, 0), (0, rows - L), (0, rows - L)))
                sel = jnp.eye(per, dtype=bool)[None, :, None, :, None]
                w_bd = jnp.where(sel, w_small[:, None, :, None, :], 0.0).reshape(
                    GMLP_GROUPS, GMLP_CHUNK, GMLP_CHUNK)
                b_small = jnp.tile(jnp.pad(bs[:, :L], ((0, 0), (0, rows - L))), (1, per))
                bias_full = jnp.repeat(b_small.T, GMLP_GROUP_DIM, axis=1)
                gm, vn = _gmlp(pp.reshape(B * rows, W), wts["gmlp_norm"][o_], w_bd, bias_full,
                               u_col=3, vg_col=4)
                gm = gm.reshape(B, rows, -1)[:, :L].reshape(M, -1)
                vn = vn.reshape(B, rows, -1)[:, :L]
            else:
                chunk = math.gcd(L, GLA_CHUNK)
                s0 = jnp.zeros((B, GLA_HEADS, GLA_DK, GLA_DV), F32)
                og, s_new = _gla(p.reshape(B, L, W), s0, wts["gla_w_gate"][o_], wts["gla_b_gate"][o_],
                                 wts["gla_norm"][o_], chunk=chunk, n_valid=chunk, gl_col=gl_col)
                og = og.reshape(M, -1)
                bias_full = jnp.repeat(bs.T, GMLP_GROUP_DIM, axis=1)
                gm, vn = _gmlp(p, wts["gmlp_norm"][o_], ws, bias_full, u_col=3, vg_col=4)
                vn = vn.reshape(B, L, -1)
            x = _mm_res(og, gm, wts["w_out_odd"][o_], x)
            glas.append(s_new)
            gvs.append(vn)
        qm = _nmm(x, wts["norm_mem"][l], wts["mem_wq"][l], out_dtype=BF16)
        mw = qm.shape[1]
        if sampling:
            qm = _pad_rows(qm.reshape(B, L, mw), SAMPLE_ROWS)
            om = _mem_attend(qm, mem_k[l], mem_v[l])[:, :L].reshape(M, mw)
        else:
            om = _mem_attend(qm.reshape(B, L, mw), mem_k[l], mem_v[l]).reshape(M, mw)
        x = _mm_res(om, None, wts["mem_wo"][l], x)
        hdn = _nmm(x, wts["norm_ffn"][l], wts["ffn_w1"][l], act="relu2", out_dtype=BF16)
        x = _mm_res(hdn, None, wts["ffn_w2"][l], x)
    y = _rmsnorm(x, wts["norm_final"])
    return y, jnp.stack(ks), jnp.stack(vs), jnp.stack(pools), jnp.stack(glas), jnp.stack(gvs)


def _prep_w_in_odd(w):
    main = 2 * GLA_HEADS * GLA_DK + 2 * GLA_HEADS * GLA_DV
    gate = w[:, :, main:main + GLA_RANK]
    rest = w[:, :, main + GLA_RANK:]
    gate = jnp.pad(gate, ((0, 0), (0, 0), (0, 256 - GLA_RANK)))
    return jnp.concatenate([w[:, :, :main], rest, gate], axis=-1)


def kernel(x_prompt, x_sample, mem_prompt, cache_sb_k, cache_sb_v, page_table, state_pool, state_gla, cache_mem_k, cache_mem_v, norm_mix, norm_mem, norm_memkv, norm_ffn, norm_final, w_in_even, w_out_even, pool_w, pool_scale, sb_bias, w_in_odd, w_out_odd, gla_w_gate, gla_b_gate, gla_norm, gmlp_norm, gmlp_ws, gmlp_bs, mem_wq, mem_wk, mem_wv, mem_wo, ffn_w1, ffn_w2):
    bf = lambda w: w.astype(BF16)
    wts = {
        "norm_mix": norm_mix, "norm_mem": norm_mem, "norm_ffn": norm_ffn, "norm_final": norm_final,
        "w_in_even": bf(w_in_even), "w_out_even": bf(w_out_even), "pool_w": pool_w,
        "pool_scale": pool_scale, "sb_bias": sb_bias,
        "w_in_odd": bf(_prep_w_in_odd(w_in_odd)), "w_out_odd": bf(w_out_odd),
        "gla_w_gate": gla_w_gate, "gla_b_gate": gla_b_gate, "gla_norm": gla_norm,
        "gmlp_norm": gmlp_norm, "gmlp_ws": gmlp_ws, "gmlp_bs": gmlp_bs,
        "mem_wq": bf(mem_wq), "mem_wo": bf(mem_wo), "ffn_w1": bf(ffn_w1), "ffn_w2": bf(ffn_w2),
    }
    depth = norm_mix.shape[0]
    B, L, D = x_prompt.shape
    n_mem = mem_prompt.shape[1]
    mp = mem_prompt.reshape(B * n_mem, D)
    mkp = jnp.stack([_nmm(mp, norm_memkv[l], bf(mem_wk[l])) for l in range(depth)])
    mvp = jnp.stack([_nmm(mp, norm_memkv[l], bf(mem_wv[l])) for l in range(depth)])
    mem_shape = (depth, B, n_mem, MEM_HEADS, MEM_HEAD_DIM)
    mkp = mkp.reshape(mem_shape)
    mvp = mvp.reshape(mem_shape)
    y_p, sbk_p, sbv_p, pool_p, gla_p, _ = _trunk(
        x_prompt.reshape(B * L, D), B, L, 0, mkp, mvp, None, None, None, None, None, wts)
    Bs, Ls, _ = x_sample.shape
    past_len = page_table.shape[1] * PAGE_SIZE
    y_s, sbk_s, sbv_s, pool_s, gla_s, gv_s = _trunk(
        x_sample.reshape(Bs * Ls, D), Bs, Ls, past_len, cache_mem_k, cache_mem_v, state_pool, state_gla,
        page_table, cache_sb_k, cache_sb_v, wts)
    return (y_p.reshape(B, L, D), y_s.reshape(Bs, Ls, D), sbk_p, sbv_p, pool_p, gla_p,
            mkp, mvp, sbk_s, sbv_s, pool_s, gla_s, gv_s)
```

```python
import functools
import math

import jax
import jax.numpy as jnp
from jax import lax
from jax.experimental import pallas as pl
from jax.experimental.pallas import tpu as pltpu

F32 = jnp.float32
BF16 = jnp.bfloat16
EPS = 1e-6

LANES = 128
PAGE_SIZE = 128
POOL_WINDOWS = (2, 4, 8, 16)
POOL_HALO = 16
POOL_GROUP_DIM = 128
SB_HEAD_DIM = 128
SB_BLOCK = 256
SB_QBLOCK = 1024
SB_SLOTS = 16
GLA_HEADS = 4
GLA_DK = 128
GLA_DV = 256
GLA_RANK = 16
GLA_TAU = 16.0
GLA_CHUNK = 32
GMLP_GROUPS = 4
GMLP_GROUP_DIM = 256
GMLP_CHUNK = 128
MEM_HEADS = 4
MEM_HEAD_DIM = 128
SAMPLE_ROWS = 16
VMEM_LIMIT = 56 * 1024 * 1024


def _params(n_axes):
    return pltpu.CompilerParams(
        dimension_semantics=("arbitrary",) * n_axes, vmem_limit_bytes=VMEM_LIMIT)


def _tile(n, candidates):
    for c in candidates:
        if n % c == 0:
            return c
    return n


def _softplus(z):
    return jnp.maximum(z, 0.0) + jnp.log(1.0 + jnp.exp(-jnp.abs(z)))


def _split_bf16(x):
    hi = x.astype(BF16)
    lo = (x - hi.astype(F32)).astype(BF16)
    return hi, lo


def _nmm_kernel(x_ref, g_ref, w_ref, o_ref, xn_ref, *, act, rows):
    @pl.when(pl.program_id(1) == 0)
    def _():
        g = g_ref[...]

        def body(c, carry):
            r0 = pl.multiple_of(c * rows, rows)
            x = x_ref[pl.ds(r0, rows), :]
            ms = jnp.mean(x * x, axis=-1, keepdims=True)
            xn_ref[pl.ds(r0, rows), :] = (x * lax.rsqrt(ms + EPS) * g).astype(BF16)
            return carry

        lax.fori_loop(0, x_ref.shape[0] // rows, body, 0)

    y = jnp.dot(xn_ref[...], w_ref[...], preferred_element_type=F32)
    if act == "relu2":
        y = jnp.square(jnp.maximum(y, 0.0))
    o_ref[...] = y.astype(o_ref.dtype)


def _nmm(x, g, w, *, act=None, out_dtype=F32):
    M, K = x.shape
    N = w.shape[1]
    tm = _tile(M, (1024, 512, 256))
    tn = _tile(N, (1024, 768, 512, 256, 128))
    rows = _tile(tm, (128, 64, 32, 16, 8))
    return pl.pallas_call(
        functools.partial(_nmm_kernel, act=act, rows=rows),
        grid=(M // tm, N // tn),
        in_specs=[
            pl.BlockSpec((tm, K), lambda i, j: (i, 0)),
            pl.BlockSpec((1, K), lambda i, j: (0, 0)),
            pl.BlockSpec((K, tn), lambda i, j: (0, j)),
        ],
        out_specs=pl.BlockSpec((tm, tn), lambda i, j: (i, j)),
        out_shape=jax.ShapeDtypeStruct((M, N), out_dtype),
        scratch_shapes=[pltpu.VMEM((tm, K), BF16)],
        compiler_params=_params(2),
        name="norm_matmul",
    )(x, g.reshape(1, K), w)


def _mmres_kernel(*refs, n1, nk, two):
    if two:
        a1_ref, a2_ref, w_ref, res_ref, o_ref = refs
    else:
        a1_ref, w_ref, res_ref, o_ref = refs
        a2_ref = None
    k = pl.program_id(2)

    @pl.when(k == 0)
    def _():
        o_ref[...] = res_ref[...]

    @pl.when(k < n1)
    def _():
        o_ref[...] += jnp.dot(a1_ref[...], w_ref[...], preferred_element_type=F32)

    if two:
        @pl.when(k >= n1)
        def _():
            o_ref[...] += jnp.dot(a2_ref[...], w_ref[...], preferred_element_type=F32)


def _mm_res(a1, a2, w, res):
    M, K1 = a1.shape
    K2 = 0 if a2 is None else a2.shape[1]
    N = w.shape[1]
    tm = _tile(M, (1024, 512, 256))
    tn = _tile(N, (1024, 512, 256))
    tk = _tile(math.gcd(K1, K2) if K2 else K1, (1024, 512, 256, 128))
    n1 = K1 // tk
    nk = (K1 + K2) // tk
    two = a2 is not None
    in_specs = [pl.BlockSpec((tm, tk), lambda i, j, k: (i, jnp.minimum(k, n1 - 1)))]
    args = [a1]
    if two:
        in_specs.append(pl.BlockSpec((tm, tk), lambda i, j, k: (i, jnp.maximum(k - n1, 0))))
        args.append(a2)
    in_specs += [
        pl.BlockSpec((tk, tn), lambda i, j, k: (k, j)),
        pl.BlockSpec((tm, tn), lambda i, j, k: (i, j)),
    ]
    args += [w, res]
    return pl.pallas_call(
        functools.partial(_mmres_kernel, n1=n1, nk=nk, two=two),
        grid=(M // tm, N // tn, nk),
        in_specs=in_specs,
        out_specs=pl.BlockSpec((tm, tn), lambda i, j, k: (i, j)),
        out_shape=jax.ShapeDtypeStruct((M, N), F32),
        compiler_params=_params(3),
        name="matmul_residual",
    )(*args)


def _rms_kernel(x_ref, g_ref, o_ref):
    x = x_ref[...]
    ms = jnp.mean(x * x, axis=-1, keepdims=True)
    o_ref[...] = x * lax.rsqrt(ms + EPS) * g_ref[...]


def _rmsnorm(x, g):
    M, K = x.shape
    tm = _tile(M, (256, 128, 64, 8))
    return pl.pallas_call(
        _rms_kernel,
        grid=(M // tm,),
        in_specs=[pl.BlockSpec((tm, K), lambda i: (i, 0)), pl.BlockSpec((1, K), lambda i: (0, 0))],
        out_specs=pl.BlockSpec((tm, K), lambda i: (i, 0)),
        out_shape=jax.ShapeDtypeStruct((M, K), F32),
        compiler_params=_params(1),
        name="final_rmsnorm",
    )(x, g.reshape(1, K))


def _pool_kernel(ext_ref, w_ref, scale_ref, o_ref, diff_ref, *, bb, lc, pos_base, pos_stride):
    blk = pl.program_id(0)
    g = pl.program_id(1)
    win = jnp.where(g == 0, POOL_WINDOWS[0],
                    jnp.where(g == 1, POOL_WINDOWS[1],
                              jnp.where(g == 2, POOL_WINDOWS[2], POOL_WINDOWS[3])))
    t_iota = lax.broadcasted_iota(jnp.int32, (lc, LANES), 0)

    def body(b, carry):
        def e(i):
            return ext_ref[b, pl.ds(POOL_HALO - i, lc), :]

        cur = e(0)
        s2 = cur + e(1)
        s4 = s2 + e(2) + e(3)
        s8 = s4 + e(4) + e(5) + e(6) + e(7)
        s16 = s8 + e(8) + e(9) + e(10) + e(11) + e(12) + e(13) + e(14) + e(15)
        ws = jnp.where(g == 0, s2, jnp.where(g == 1, s4, jnp.where(g == 2, s8, s16)))
        pos = pos_base + (blk * bb + b) * pos_stride + t_iota
        cnt = jnp.minimum(win, pos + 1).astype(F32)
        r0 = pl.multiple_of(b * lc, lc)
        diff_ref[pl.ds(r0, lc), :] = ws / cnt - cur
        return carry

    lax.fori_loop(0, bb, body, 0)
    out = jnp.dot(diff_ref[...].astype(BF16), w_ref[0].astype(BF16), preferred_element_type=F32)
    o_ref[...] = (out * scale_ref[...]).astype(o_ref.dtype)


def _pool_mix(ext, w, scale, *, pos_base, pos_stride):
    NB, rows, width = ext.shape
    lc = rows - POOL_HALO
    bb = NB if NB * lc <= 2048 else _tile(NB, (4, 2, 1))
    G = width // POOL_GROUP_DIM
    return pl.pallas_call(
        functools.partial(_pool_kernel, bb=bb, lc=lc, pos_base=pos_base, pos_stride=pos_stride),
        grid=(NB // bb, G),
        in_specs=[
            pl.BlockSpec((bb, rows, POOL_GROUP_DIM), lambda i, g: (i, 0, g)),
            pl.BlockSpec((1, POOL_GROUP_DIM, POOL_GROUP_DIM), lambda i, g: (g, 0, 0)),
            pl.BlockSpec((1, POOL_GROUP_DIM), lambda i, g: (0, g)),
        ],
        out_specs=pl.BlockSpec((bb * lc, POOL_GROUP_DIM), lambda i, g: (i, g)),
        out_shape=jax.ShapeDtypeStruct((NB * lc, width), BF16),
        scratch_shapes=[pltpu.VMEM((bb * lc, POOL_GROUP_DIM), F32)],
        compiler_params=_params(2),
        name="pool_mix",
    )(ext, w, scale.reshape(1, width))


def _suffix_matrix(n):
    r = jnp.arange(n)[:, None]
    c = jnp.arange(n)[None, :]
    u = (r > c).astype(BF16)
    uj = jnp.concatenate([u, jnp.ones((n, n), BF16)], axis=1)
    return jnp.concatenate([uj, uj], axis=0)


def _sb_prompt_kernel(bias_ref, q_ref, k_ref, v_ref, uj_ref, o_ref, acc_ref, run_ref, *, scale):
    h = pl.program_id(0)
    i = pl.program_id(1)
    tk = SB_BLOCK
    half = tk // 2
    n_sub = SB_QBLOCK // tk
    bias = bias_ref[h]
    uj = uj_ref[...]
    row = lax.broadcasted_iota(jnp.int32, (tk, tk), 0)
    col = lax.broadcasted_iota(jnp.int32, (tk, tk), 1)
    acc_ref[...] = jnp.zeros_like(acc_ref)
    run_ref[...] = jnp.zeros_like(run_ref)

    def suffix(lk_half):
        hi, lo = _split_bf16(lk_half)
        r = jnp.dot(jnp.concatenate([hi, lo], axis=1), uj, preferred_element_type=F32)
        return r[:, :half], r[:, half:]

    def step(j, inner):
        start = pl.multiple_of(j * tk, tk)
        kb = k_ref[pl.ds(start, tk), :].astype(BF16)
        vb = v_ref[pl.ds(start, tk), :].astype(BF16)
        for s in range(n_sub):
            if inner is not None and inner > s:
                continue
            masked = inner is not None and inner == s
            rows = slice(s * tk, (s + 1) * tk)
            q = q_ref[rows, :].astype(BF16)
            z = lax.dot_general(q, kb, (((1,), (1,)), ((), ())), preferred_element_type=F32)
            z = z * scale + bias
            sp = _softplus(z)
            if masked:
                mask = col < row
                lk = jnp.where(mask, -sp, 0.0)
            else:
                lk = -sp
            la0, t0 = suffix(lk[:, :half])
            la1, t1 = suffix(lk[:, half:])
            run = run_ref[rows, :]
            la = jnp.concatenate([la0 + (t1 + run), la1 + run], axis=1)
            wgt = jnp.exp(z - sp + la)
            if masked:
                wgt = jnp.where(mask, wgt, 0.0)
            acc_ref[rows, :] += jnp.dot(wgt.astype(BF16), vb, preferred_element_type=F32)
            run_ref[rows, :] = run + t0 + t1

    for d in range(n_sub):
        step(i * n_sub + (n_sub - 1 - d), n_sub - 1 - d)

    def body(jj, carry):
        step(i * n_sub - 1 - jj, None)
        return carry

    lax.fori_loop(0, i * n_sub, body, 0)
    o_ref[...] = acc_ref[...].astype(o_ref.dtype)


def _sb_prompt(p, bias, *, q_col, k_col, v_col, heads):
    L = p.shape[0]
    tq = SB_QBLOCK
    tk = SB_BLOCK
    scale = SB_HEAD_DIM ** -0.5
    return pl.pallas_call(
        functools.partial(_sb_prompt_kernel, scale=scale),
        grid=(heads, L // tq),
        in_specs=[
            pl.BlockSpec(memory_space=pltpu.SMEM),
            pl.BlockSpec((tq, SB_HEAD_DIM), lambda h, i: (i, q_col + h)),
            pl.BlockSpec((L, SB_HEAD_DIM), lambda h, i: (0, k_col + h)),
            pl.BlockSpec((L, SB_HEAD_DIM), lambda h, i: (0, v_col + h)),
            pl.BlockSpec((tk, tk), lambda h, i: (0, 0)),
        ],
        out_specs=pl.BlockSpec((tq, SB_HEAD_DIM), lambda h, i: (i, h)),
        out_shape=jax.ShapeDtypeStruct((L, heads * SB_HEAD_DIM), BF16),
        scratch_shapes=[pltpu.VMEM((tq, SB_HEAD_DIM), F32), pltpu.VMEM((tq, SB_BLOCK // 2), F32)],
        compiler_params=_params(2),
        name="sb_prompt",
    )(bias, p, p, p, _suffix_matrix(SB_BLOCK // 2))


def _sb_sample_kernel(pt_ref, bias_ref, q_ref, knew_ref, vnew_ref, *rest, pps, n_new, heads, scale):
    k_refs = rest[:pps]
    v_refs = rest[pps:2 * pps]
    uj_ref, o_ref, acc_ref, run_ref, kpad_ref, vpad_ref = rest[2 * pps:]
    s = pl.program_id(1)
    uj = uj_ref[...]
    qs = [q_ref[0, h].astype(BF16) for h in range(heads)]

    def process(kref, vref, mask):
        zs = []
        for h in range(heads):
            kh = kref[h].astype(BF16)
            zh = lax.dot_general(qs[h], kh, (((1,), (1,)), ((), ())), preferred_element_type=F32)
            zs.append(zh * scale + bias_ref[h])
        z = jnp.concatenate(zs, axis=0)
        sp = _softplus(z)
        lk = -sp if mask is None else jnp.where(mask, -sp, 0.0)
        hi, lo = _split_bf16(lk)
        r = jnp.dot(jnp.concatenate([hi, lo], axis=1), uj, preferred_element_type=F32)
        run = run_ref[...]
        w = jnp.exp(z - sp + r[:, :PAGE_SIZE] + run)
        if mask is not None:
            w = jnp.where(mask, w, 0.0)
        for h in range(heads):
            wh = w[h * SB_SLOTS:(h + 1) * SB_SLOTS].astype(BF16)
            acc_ref[h] += jnp.dot(wh, vref[h].astype(BF16), preferred_element_type=F32)
        run_ref[...] = run + r[:, PAGE_SIZE:]

    @pl.when(s == 0)
    def _():
        acc_ref[...] = jnp.zeros_like(acc_ref)
        run_ref[...] = jnp.zeros_like(run_ref)
        kpad_ref[...] = jnp.zeros_like(kpad_ref)
        vpad_ref[...] = jnp.zeros_like(vpad_ref)
        kpad_ref[:, 0:8, :] = knew_ref[0]
        vpad_ref[:, 0:8, :] = vnew_ref[0]
        shape = (heads * SB_SLOTS, PAGE_SIZE)
        t = lax.broadcasted_iota(jnp.int32, shape, 0) % SB_SLOTS
        key = lax.broadcasted_iota(jnp.int32, shape, 1)
        process(kpad_ref, vpad_ref, jnp.logical_and(key < t, key < n_new))

    for r in range(pps):
        process(k_refs[r], v_refs[r], None)

    @pl.when(s == pl.num_programs(1) - 1)
    def _():
        o_ref[0] = acc_ref[...]


def _sb_sample(q, knew, vnew, cache_k, cache_v, e, page_table, bias):
    B, L, H, Dh = q.shape
    NP = page_table.shape[1]
    pps = math.gcd(NP, 4)
    q16 = jnp.pad(jnp.transpose(q, (0, 2, 1, 3)), ((0, 0), (0, 0), (0, SB_SLOTS - L), (0, 0)))
    kn = jnp.pad(jnp.transpose(knew, (0, 2, 1, 3)), ((0, 0), (0, 0), (0, 8 - L), (0, 0)))
    vn = jnp.pad(jnp.transpose(vnew, (0, 2, 1, 3)), ((0, 0), (0, 0), (0, 8 - L), (0, 0)))
    cache_k = jnp.transpose(cache_k, (0, 1, 3, 2, 4))
    cache_v = jnp.transpose(cache_v, (0, 1, 3, 2, 4))

    def page_spec(rr):
        return pl.BlockSpec((None, None, H, PAGE_SIZE, Dh),
                            lambda b, s, pt: (e, pt[b, NP - 1 - (s * pps + rr)], 0, 0, 0))

    seq4 = lambda b, s, pt: (b, 0, 0, 0)
    grid_spec = pltpu.PrefetchScalarGridSpec(
        num_scalar_prefetch=1,
        grid=(B, NP // pps),
        in_specs=[
            pl.BlockSpec(memory_space=pltpu.SMEM),
            pl.BlockSpec((1, H, SB_SLOTS, Dh), seq4),
            pl.BlockSpec((1, H, 8, Dh), seq4),
            pl.BlockSpec((1, H, 8, Dh), seq4),
            *[page_spec(rr) for rr in range(pps)],
            *[page_spec(rr) for rr in range(pps)],
            pl.BlockSpec((2 * PAGE_SIZE, 2 * PAGE_SIZE), lambda b, s, pt: (0, 0)),
        ],
        out_specs=pl.BlockSpec((1, H, SB_SLOTS, Dh), seq4),
        scratch_shapes=[
            pltpu.VMEM((H, SB_SLOTS, Dh), F32),
            pltpu.VMEM((H * SB_SLOTS, PAGE_SIZE), F32),
            pltpu.VMEM((H, PAGE_SIZE, Dh), F32),
            pltpu.VMEM((H, PAGE_SIZE, Dh), F32),
        ],
    )
    out = pl.pallas_call(
        functools.partial(_sb_sample_kernel, pps=pps, n_new=L, heads=H, scale=Dh ** -0.5),
        grid_spec=grid_spec,
        out_shape=jax.ShapeDtypeStruct((B, H, SB_SLOTS, Dh), F32),
        compiler_params=_params(2),
        name="sb_sample",
    )(page_table, bias.astype(F32), q16, kn, vn, *([cache_k] * pps), *([cache_v] * pps),
      _suffix_matrix(PAGE_SIZE))
    return jnp.transpose(out[:, :, :L], (0, 2, 1, 3)).reshape(B * L, H * Dh).astype(BF16)


def _mem_kernel(q_ref, k_ref, v_ref, o_ref, *, scale):
    for h in range(MEM_HEADS):
        sl = slice(h * MEM_HEAD_DIM, (h + 1) * MEM_HEAD_DIM)
        q = q_ref[0, :, sl]
        k = k_ref[0, :, h, :].astype(BF16)
        v = v_ref[0, :, h, :].astype(BF16)
        s = lax.dot_general(q, k, (((1,), (1,)), ((), ())), preferred_element_type=F32) * scale
        m = jnp.max(s, axis=-1, keepdims=True)
        e = jnp.exp(s - m)
        p = e / jnp.sum(e, axis=-1, keepdims=True)
        o_ref[0, :, sl] = jnp.dot(p.astype(BF16), v, preferred_element_type=F32).astype(o_ref.dtype)


def _mem_attend(q, mk, mv):
    B, L, W = q.shape
    n_mem = mk.shape[1]
    tq = _tile(L, (512, 256, 128, 64, 32, 16))
    return pl.pallas_call(
        functools.partial(_mem_kernel, scale=MEM_HEAD_DIM ** -0.5),
        grid=(B, L // tq),
        in_specs=[
            pl.BlockSpec((1, tq, W), lambda b, i: (b, i, 0)),
            pl.BlockSpec((1, n_mem, MEM_HEADS, MEM_HEAD_DIM), lambda b, i: (b, 0, 0, 0)),
            pl.BlockSpec((1, n_mem, MEM_HEADS, MEM_HEAD_DIM), lambda b, i: (b, 0, 0, 0)),
        ],
        out_specs=pl.BlockSpec((1, tq, W), lambda b, i: (b, i, 0)),
        out_shape=jax.ShapeDtypeStruct((B, L, W), BF16),
        compiler_params=_params(2),
        name="mem_attend",
    )(q, mk, mv)


def _gla_kernel(q_ref, k_ref, v_ref, r_ref, gl_ref, wg_ref, bg_ref, gn_ref, lt_ref, s0_ref,
                o_ref, sout_ref, s_ref, *, chunk, n_valid, n_chunks):
    i = pl.program_id(1)
    tm = chunk * n_chunks

    @pl.when(i == 0)
    def _():
        s_ref[...] = s0_ref[0]

    x = jnp.dot(gl_ref[0].astype(BF16), wg_ref[...].astype(BF16),
                preferred_element_type=F32) + bg_ref[...]
    la = -_softplus(-x) / GLA_TAU
    if n_valid < chunk:
        row = lax.broadcasted_iota(jnp.int32, la.shape, 0) % chunk
        la = jnp.where(row < n_valid, la, 0.0)
    hi, lo = _split_bf16(la)
    bcum = jnp.dot(lt_ref[...], jnp.concatenate([hi, lo], axis=0), preferred_element_type=F32)
    tri = (lax.broadcasted_iota(jnp.int32, (chunk, chunk), 0)
           >= lax.broadcasted_iota(jnp.int32, (chunk, chunk), 1))
    gn = gn_ref[...]
    qscale = GLA_DK ** -0.5
    for c in range(n_chunks):
        rows = slice(c * chunk, (c + 1) * chunk)
        for h in range(GLA_HEADS):
            kcols = slice(h * GLA_DK, (h + 1) * GLA_DK)
            vcols = slice(h * GLA_DV, (h + 1) * GLA_DV)
            b = bcum[rows, kcols]
            bl = b[chunk - 1:chunk, :]
            qh = q_ref[0, rows, kcols] * qscale
            kh = k_ref[0, rows, kcols]
            vh = v_ref[0, rows, vcols].astype(BF16)
            qd = (qh * jnp.exp(b)).astype(BF16)
            kd = (kh * jnp.exp(-b)).astype(BF16)
            sc = lax.dot_general(qd, kd, (((1,), (1,)), ((), ())), preferred_element_type=F32)
            sc = jnp.where(tri, sc, 0.0).astype(BF16)
            st = s_ref[h]
            o = (jnp.dot(sc, vh, preferred_element_type=F32)
                 + jnp.dot(qd, st.astype(BF16), preferred_element_type=F32))
            ku = (kh * jnp.exp(bl - b)).astype(BF16)
            upd = lax.dot_general(ku, vh, (((0,), (0,)), ((), ())), preferred_element_type=F32)
            decay = jnp.exp(jnp.broadcast_to(bl, (GLA_DK, GLA_DK)).T)
            s_ref[h] = jnp.concatenate([decay, decay], axis=1) * st + upd
            ms = jnp.mean(o * o, axis=-1, keepdims=True)
            rh = r_ref[0, rows, vcols]
            gate = rh / (1.0 + jnp.exp(-rh))
            o_ref[0, rows, vcols] = (o * lax.rsqrt(ms + EPS) * gn * gate).astype(o_ref.dtype)

    @pl.when(i == pl.num_programs(1) - 1)
    def _():
        sout_ref[0] = s_ref[...]


def _gla(p, s0, w_gate, b_gate, gnorm, *, chunk, n_valid, gl_col):
    B, Lp, W = p.shape
    n_chunks = _tile(Lp // chunk, (4, 2, 1))
    tm = chunk * n_chunks
    kw = GLA_HEADS * GLA_DK
    vw = GLA_HEADS * GLA_DV
    glw = 256
    wg = jnp.pad(w_gate, ((0, glw - w_gate.shape[0]), (0, 0)))
    r = jnp.arange(tm)
    lt = ((r[:, None] >= r[None, :]) & (r[:, None] // chunk == r[None, :] // chunk)).astype(BF16)
    lt2 = jnp.concatenate([lt, lt], axis=1)
    o, s_out = pl.pallas_call(
        functools.partial(_gla_kernel, chunk=chunk, n_valid=n_valid, n_chunks=n_chunks),
        grid=(B, Lp // tm),
        in_specs=[
            pl.BlockSpec((1, tm, kw), lambda b, i: (b, i, 0)),
            pl.BlockSpec((1, tm, kw), lambda b, i: (b, i, 1)),
            pl.BlockSpec((1, tm, vw), lambda b, i: (b, i, 1)),
            pl.BlockSpec((1, tm, vw), lambda b, i: (b, i, 2)),
            pl.BlockSpec((1, tm, glw), lambda b, i: (b, i, gl_col // glw)),
            pl.BlockSpec((glw, kw), lambda b, i: (0, 0)),
            pl.BlockSpec((1, kw), lambda b, i: (0, 0)),
            pl.BlockSpec((1, GLA_DV), lambda b, i: (0, 0)),
            pl.BlockSpec((tm, 2 * tm), lambda b, i: (0, 0)),
            pl.BlockSpec((1, GLA_HEADS, GLA_DK, GLA_DV), lambda b, i: (b, 0, 0, 0)),
        ],
        out_specs=[
            pl.BlockSpec((1, tm, vw), lambda b, i: (b, i, 0)),
            pl.BlockSpec((1, GLA_HEADS, GLA_DK, GLA_DV), lambda b, i: (b, 0, 0, 0)),
        ],
        out_shape=[
            jax.ShapeDtypeStruct((B, Lp, vw), BF16),
            jax.ShapeDtypeStruct((B, GLA_HEADS, GLA_DK, GLA_DV), F32),
        ],
        scratch_shapes=[pltpu.VMEM((GLA_HEADS, GLA_DK, GLA_DV), F32)],
        compiler_params=_params(2),
        name="gla",
    )(p, p, p, p, p, wg, b_gate.reshape(1, kw), gnorm.reshape(1, GLA_DV), lt2, s0)
    return o, s_out


def _gmlp_kernel(u_ref, vg_ref, gn_ref, w_ref, b_ref, o_ref, vn_ref):
    vg = vg_ref[...]
    ms = jnp.mean(vg * vg, axis=-1, keepdims=True)
    vn = vg * lax.rsqrt(ms + EPS) * gn_ref[...]
    vn_ref[...] = vn
    n = w_ref.shape[1]
    tri = (lax.broadcasted_iota(jnp.int32, (n, n), 0) >= lax.broadcasted_iota(jnp.int32, (n, n), 1))
    for g in range(GMLP_GROUPS):
        cols = slice(g * GMLP_GROUP_DIM, (g + 1) * GMLP_GROUP_DIM)
        w = jnp.where(tri, w_ref[g], 0.0).astype(BF16)
        mixed = jnp.dot(w, vn[:, cols].astype(BF16), preferred_element_type=F32) + b_ref[:, cols]
        o_ref[:, cols] = (u_ref[:, cols] * mixed).astype(o_ref.dtype)


def _gmlp(p, gnorm, w, bias_full, *, u_col, vg_col):
    M = p.shape[0]
    n = GMLP_CHUNK
    width = GMLP_GROUPS * GMLP_GROUP_DIM
    return pl.pallas_call(
        _gmlp_kernel,
        grid=(M // n,),
        in_specs=[
            pl.BlockSpec((n, width), lambda i: (i, u_col)),
            pl.BlockSpec((n, width), lambda i: (i, vg_col)),
            pl.BlockSpec((1, width), lambda i: (0, 0)),
            pl.BlockSpec((GMLP_GROUPS, n, n), lambda i: (0, 0, 0)),
            pl.BlockSpec((n, width), lambda i: (0, 0)),
        ],
        out_specs=[pl.BlockSpec((n, width), lambda i: (i, 0)), pl.BlockSpec((n, width), lambda i: (i, 0))],
        out_shape=[jax.ShapeDtypeStruct((M, width), BF16), jax.ShapeDtypeStruct((M, width), F32)],
        compiler_params=_params(1),
        name="gmlp",
    )(p, p, gnorm.reshape(1, width), w, bias_full)


def _pad_rows(a, rows):
    return jnp.pad(a, ((0, 0), (0, rows - a.shape[1]), (0, 0)))


def _trunk(x, B, L, pos0, mem_k, mem_v, pool_state, gla_state, page_table, cache_sb_k, cache_sb_v, wts):
    sampling = page_table is not None
    M, D = x.shape
    depth = wts["norm_mix"].shape[0]
    pools, glas, ks, vs, gvs = [], [], [], [], []
    for l in range(depth):
        if l % 2 == 0:
            e = l // 2
            pool_w = wts["pool_w"][e]
            pw = pool_w.shape[0] * pool_w.shape[1]
            sbw = (wts["w_in_even"][e].shape[1] - pw) // 3
            heads = sbw // SB_HEAD_DIM
            p = _nmm(x, wts["norm_mix"][l], wts["w_in_even"][e])
            a = p[:, :pw].reshape(B, L, pw)
            k = p[:, pw + sbw:pw + 2 * sbw].reshape(B, L, heads, SB_HEAD_DIM)
            v = p[:, pw + 2 * sbw:].reshape(B, L, heads, SB_HEAD_DIM)
            if sampling:
                buf = pool_state[e]
                ext = jnp.concatenate(
                    [jnp.zeros((B, POOL_HALO - buf.shape[1], pw), F32), buf, _pad_rows(a, 8)], axis=1)
                pool_out = _pool_mix(ext, pool_w, wts["pool_scale"][e], pos_base=pos0, pos_stride=0)
                pool_out = pool_out.reshape(B, 8, pw)[:, :L].reshape(M, pw)
                new_buf = jnp.concatenate([buf, a], axis=1)[:, -buf.shape[1]:]
                q = p[:, pw:pw + sbw].reshape(B, L, heads, SB_HEAD_DIM)
                o = _sb_sample(q, k, v, cache_sb_k, cache_sb_v, e, page_table, wts["sb_bias"][e])
            else:
                lc = _tile(L, (256, 128, 64, 32, 16, 8))
                nb = L // lc
                full = jnp.concatenate([jnp.zeros((POOL_HALO, pw), F32), a[0]], axis=0)
                ext = jnp.stack([full[c * lc:c * lc + lc + POOL_HALO] for c in range(nb)])
                pool_out = _pool_mix(ext, pool_w, wts["pool_scale"][e], pos_base=0, pos_stride=lc)
                hist = POOL_HALO - 1
                new_buf = jnp.concatenate([jnp.zeros((B, hist, pw), F32), a], axis=1)[:, -hist:]
                o = _sb_prompt(p, wts["sb_bias"][e].astype(F32), q_col=pw // SB_HEAD_DIM,
                               k_col=(pw + sbw) // SB_HEAD_DIM, v_col=(pw + 2 * sbw) // SB_HEAD_DIM,
                               heads=heads)
            x = _mm_res(pool_out, o, wts["w_out_even"][e], x)
            pools.append(new_buf)
            ks.append(k)
            vs.append(v)
        else:
            o_ = l // 2
            p = _nmm(x, wts["norm_mix"][l], wts["w_in_odd"][o_])
            W = p.shape[1]
            gl_col = 2 * GLA_HEADS * GLA_DK + 2 * GLA_HEADS * GLA_DV + 2 * GMLP_GROUPS * GMLP_GROUP_DIM
            ws = wts["gmlp_ws"][o_]
            bs = wts["gmlp_bs"][o_]
            if sampling:
                rows = SAMPLE_ROWS
                pp = _pad_rows(p.reshape(B, L, W), rows)
                og, s_new = _gla(pp, gla_state[o_], wts["gla_w_gate"][o_], wts["gla_b_gate"][o_],
                                 wts["gla_norm"][o_], chunk=rows, n_valid=L, gl_col=gl_col)
                og = og[:, :L].reshape(M, -1)
                per = GMLP_CHUNK // rows
                w_small = jnp.pad(ws[:, :L, :L], ((0, 0), (0, rows - L), (0, rows - L)))
                sel = jnp.eye(per, dtype=bool)[None, :, None, :, None]
                w_bd = jnp.where(sel, w_small[:, None, :, None, :], 0.0).reshape(
                    GMLP_GROUPS, GMLP_CHUNK, GMLP_CHUNK)
                b_small = jnp.tile(jnp.pad(bs[:, :L], ((0, 0), (0, rows - L))), (1, per))
                bias_full = jnp.repeat(b_small.T, GMLP_GROUP_DIM, axis=1)
                gm, vn = _gmlp(pp.reshape(B * rows, W), wts["gmlp_norm"][o_], w_bd, bias_full,
                               u_col=3, vg_col=4)
                gm = gm.reshape(B, rows, -1)[:, :L].reshape(M, -1)
                vn = vn.reshape(B, rows, -1)[:, :L]
            else:
                chunk = math.gcd(L, GLA_CHUNK)
                s0 = jnp.zeros((B, GLA_HEADS, GLA_DK, GLA_DV), F32)
                og, s_new = _gla(p.reshape(B, L, W), s0, wts["gla_w_gate"][o_], wts["gla_b_gate"][o_],
                                 wts["gla_norm"][o_], chunk=chunk, n_valid=chunk, gl_col=gl_col)
                og = og.reshape(M, -1)
                bias_full = jnp.repeat(bs.T, GMLP_GROUP_DIM, axis=1)
                gm, vn = _gmlp(p, wts["gmlp_norm"][o_], ws, bias_full, u_col=3, vg_col=4)
                vn = vn.reshape(B, L, -1)
            x = _mm_res(og, gm, wts["w_out_odd"][o_], x)
            glas.append(s_new)
            gvs.append(vn)
        qm = _nmm(x, wts["norm_mem"][l], wts["mem_wq"][l], out_dtype=BF16)
        mw = qm.shape[1]
        if sampling:
            qm = _pad_rows(qm.reshape(B, L, mw), SAMPLE_ROWS)
            om = _mem_attend(qm, mem_k[l], mem_v[l])[:, :L].reshape(M, mw)
        else:
            om = _mem_attend(qm.reshape(B, L, mw), mem_k[l], mem_v[l]).reshape(M, mw)
        x = _mm_res(om, None, wts["mem_wo"][l], x)
        hdn = _nmm(x, wts["norm_ffn"][l], wts["ffn_w1"][l], act="relu2", out_dtype=BF16)
        x = _mm_res(hdn, None, wts["ffn_w2"][l], x)
    y = _rmsnorm(x, wts["norm_final"])
    return y, jnp.stack(ks), jnp.stack(vs), jnp.stack(pools), jnp.stack(glas), jnp.stack(gvs)


def _prep_w_in_odd(w):
    main = 2 * GLA_HEADS * GLA_DK + 2 * GLA_HEADS * GLA_DV
    gate = w[:, :, main:main + GLA_RANK]
    rest = w[:, :, main + GLA_RANK:]
    gate = jnp.pad(gate, ((0, 0), (0, 0), (0, 256 - GLA_RANK)))
    return jnp.concatenate([w[:, :, :main], rest, gate], axis=-1)


def kernel(x_prompt, x_sample, mem_prompt, cache_sb_k, cache_sb_v, page_table, state_pool, state_gla, cache_mem_k, cache_mem_v, norm_mix, norm_mem, norm_memkv, norm_ffn, norm_final, w_in_even, w_out_even, pool_w, pool_scale, sb_bias, w_in_odd, w_out_odd, gla_w_gate, gla_b_gate, gla_norm, gmlp_norm, gmlp_ws, gmlp_bs, mem_wq, mem_wk, mem_wv, mem_wo, ffn_w1, ffn_w2):
    bf = lambda w: w.astype(BF16)
    wts = {
        "norm_mix": norm_mix, "norm_mem": norm_mem, "norm_ffn": norm_ffn, "norm_final": norm_final,
        "w_in_even": bf(w_in_even), "w_out_even": bf(w_out_even), "pool_w": pool_w,
        "pool_scale": pool_scale, "sb_bias": sb_bias,
        "w_in_odd": bf(_prep_w_in_odd(w_in_odd)), "w_out_odd": bf(w_out_odd),
        "gla_w_gate": gla_w_gate, "gla_b_gate": gla_b_gate, "gla_norm": gla_norm,
        "gmlp_norm": gmlp_norm, "gmlp_ws": gmlp_ws, "gmlp_bs": gmlp_bs,
        "mem_wq": bf(mem_wq), "mem_wo": bf(mem_wo), "ffn_w1": bf(ffn_w1), "ffn_w2": bf(ffn_w2),
    }
    depth = norm_mix.shape[0]
    B, L, D = x_prompt.shape
    n_mem = mem_prompt.shape[1]
    mp = mem_prompt.reshape(B * n_mem, D)
    mkp = jnp.stack([_nmm(mp, norm_memkv[l], bf(mem_wk[l])) for l in range(depth)])
    mvp = jnp.stack([_nmm(mp, norm_memkv[l], bf(mem_wv[l])) for l in range(depth)])
    mem_shape = (depth, B, n_mem, MEM_HEADS, MEM_HEAD_DIM)
    mkp = mkp.reshape(mem_shape)
    mvp = mvp.reshape(mem_shape)
    y_p, sbk_p, sbv_p, pool_p, gla_p, _ = _trunk(
        x_prompt.reshape(B * L, D), B, L, 0, mkp, mvp, None, None, None, None, None, wts)
    Bs, Ls, _ = x_sample.shape
    past_len = page_table.shape[1] * PAGE_SIZE
    y_s, sbk_s, sbv_s, pool_s, gla_s, gv_s = _trunk(
        x_sample.reshape(Bs * Ls, D), Bs, Ls, past_len, cache_mem_k, cache_mem_v, state_pool, state_gla,
        page_table, cache_sb_k, cache_sb_v, wts)
    return (y_p.reshape(B, L, D), y_s.reshape(Bs, Ls, D), sbk_p, sbv_p, pool_p, gla_p,
            mkp, mvp, sbk_s, sbv_s, pool_s, gla_s, gv_s)
```
